```python
import jax, jax.numpy as jnp
from jax import lax
import numpy as np

D_MODEL = 1024
BATCH = 8
SEQ = 2048
DEPTH = 4
DEC_BATCH = 4
DEC_SEQ = 8192
PAST_LEN = 128

N_META = 16
D_MIX = D_MODEL
D_POOL = D_MIX // 4
POOL_WINDOWS = (2, 4, 8, 16)
N_POOL_GROUPS = len(POOL_WINDOWS)
D_POOL_GROUP = D_POOL // N_POOL_GROUPS
D_MLSTM = D_MIX - D_POOL
N_HEADS = 4
HEAD_DIM = D_MLSTM // N_HEADS
CHUNK = 64
CONV_K = 3
N_GATE = 4 * N_HEADS
N_IN = D_POOL + 4 * D_MLSTM + N_GATE
N_EXPERTS = 16
CAPACITY_FACTOR = 2
D_FF = D_MODEL
EPS = 1e-6
NEG = -1e30

kernel_name = "hybrid_pool_mlstm_ec_encoder"


def rmsnorm(x, g):
    xf = x.astype(jnp.float32)
    y = xf * lax.rsqrt(jnp.mean(xf * xf, axis=-1, keepdims=True) + EPS)
    return (y * g.astype(jnp.float32)).astype(x.dtype)


def pool_mixer(u, pool_w, pool_scale):
    B, L, _ = u.shape
    cs = jnp.cumsum(u.astype(jnp.float32), axis=1)
    cs = jnp.concatenate([jnp.zeros((B, 1, D_POOL), jnp.float32), cs], axis=1)
    t = np.arange(L)
    outs = []
    for gi, w in enumerate(POOL_WINDOWS):
        lo = np.clip(t - w // 2, 0, L)
        hi = np.clip(t - w // 2 + w, 0, L)
        cnt = (hi - lo).astype(np.float32)[None, :, None]
        sl = slice(gi * D_POOL_GROUP, (gi + 1) * D_POOL_GROUP)
        csg = cs[:, :, sl]
        mean = (csg[:, hi] - csg[:, lo]) / cnt
        mixed = (mean - u[:, :, sl].astype(jnp.float32)).astype(u.dtype)
        outs.append(jnp.einsum('blc,cd->bld', mixed, pool_w[gi]))
    return jnp.concatenate(outs, axis=-1) * pool_scale


def mlstm_chunkwise(q, k, v, logi, logf):
    B, H, Lp, DH = q.shape
    NC = Lp // CHUNK
    f32 = jnp.float32
    qc = q.reshape(B, H, NC, CHUNK, DH).astype(f32) * (DH ** -0.5)
    kc = k.reshape(B, H, NC, CHUNK, DH).astype(f32)
    vc = v.reshape(B, H, NC, CHUNK, DH).astype(f32)
    li = logi.reshape(B, H, NC, CHUNK)
    b = jnp.cumsum(logf.reshape(B, H, NC, CHUNK), axis=-1)
    bT = b[..., -1]
    a = bT[..., None] - b + li
    m_loc = jnp.max(a, axis=-1)
    wa = jnp.exp(a - m_loc[..., None])
    C_loc = jnp.einsum('bhnsd,bhnse->bhnde', kc * wa[..., None], vc)
    n_loc = jnp.einsum('bhns,bhnsd->bhnd', wa, kc)

    def step(carry, xs):
        C, n, m = carry
        Cl, nl, ml, bt = xs
        m_new = jnp.maximum(bt + m, ml)
        s_prev = jnp.exp(bt + m - m_new)
        s_loc = jnp.exp(ml - m_new)
        C_new = s_prev[..., None, None] * C + s_loc[..., None, None] * Cl
        n_new = s_prev[..., None] * n + s_loc[..., None] * nl
        return (C_new, n_new, m_new), (C, n, m)

    init = (jnp.zeros((B, H, DH, DH), f32), jnp.zeros((B, H, DH), f32), jnp.zeros((B, H), f32))
    xs = (jnp.moveaxis(C_loc, 2, 0), jnp.moveaxis(n_loc, 2, 0),
          jnp.moveaxis(m_loc, 2, 0), jnp.moveaxis(bT, 2, 0))
    _, (C_prev, n_prev, m_prev) = lax.scan(step, init, xs)
    C_prev = jnp.moveaxis(C_prev, 0, 2)
    n_prev = jnp.moveaxis(n_prev, 0, 2)
    m_prev = jnp.moveaxis(m_prev, 0, 2)

    mask = np.tril(np.ones((CHUNK, CHUNK), dtype=bool))
    Dlog = b[..., :, None] - b[..., None, :] + li[..., None, :]
    Dlog = jnp.where(mask, Dlog, -jnp.inf)
    inter = b + m_prev[..., None]
    m_t = jnp.maximum(inter, jnp.max(Dlog, axis=-1))
    w_inter = jnp.exp(inter - m_t)
    scores = jnp.einsum('bhntd,bhnsd->bhnts', qc, kc) * jnp.exp(Dlog - m_t[..., None])
    num = (w_inter[..., None] * jnp.einsum('bhntd,bhnde->bhnte', qc, C_prev)
           + jnp.einsum('bhnts,bhnse->bhnte', scores, vc))
    den = w_inter * jnp.einsum('bhntd,bhnd->bhnt', qc, n_prev) + jnp.sum(scores, axis=-1)
    h = num / jnp.maximum(jnp.abs(den), jnp.exp(-m_t))[..., None]
    return h.reshape(B, H, Lp, DH)


def mlstm_mixer(q, k, v, o, gates, gate_b, mh_g):
    B, L, _ = q.shape
    pad = (-L) % CHUNK
    Lp = L + pad

    def heads(t):
        t = jnp.pad(t, ((0, 0), (pad, 0), (0, 0)))
        return t.reshape(B, Lp, N_HEADS, HEAD_DIM).transpose(0, 2, 1, 3)

    g = gates.astype(jnp.float32).reshape(B, L, 4, N_HEADS) + gate_b.astype(jnp.float32)
    g = jnp.transpose(g, (2, 0, 3, 1))
    logi = jnp.pad(g[:2], ((0, 0), (0, 0), (0, 0), (pad, 0)), constant_values=NEG)
    logf = jnp.pad(jax.nn.log_sigmoid(g[2:]), ((0, 0), (0, 0), (0, 0), (pad, 0)), constant_values=0.0)
    qh, kh, vh = heads(q), heads(k), heads(v)
    h_f = mlstm_chunkwise(qh, kh, vh, logi[0], logf[0])
    h_b = jnp.flip(mlstm_chunkwise(jnp.flip(qh, 2), jnp.flip(kh, 2), jnp.flip(vh, 2),
                                   jnp.flip(logi[1], -1), jnp.flip(logf[1], -1)), 2)
    h = (h_f + h_b)[:, :, pad:]
    h = h * lax.rsqrt(jnp.mean(h * h, axis=-1, keepdims=True) + EPS)
    h = h.transpose(0, 2, 1, 3).reshape(B, L, D_MLSTM) * mh_g.astype(jnp.float32)
    return (h * jax.nn.sigmoid(o.astype(jnp.float32))).astype(q.dtype)


def centred_dwconv(x, w, b):
    C = x.shape[-1]
    y = lax.conv_general_dilated(x, w[:, None, :], window_strides=(1,), padding='SAME',
                                 dimension_numbers=('NWC', 'WIO', 'NWC'), feature_group_count=C)
    return y + b


def expert_choice_ffn(h, w_router, w_gate, w_up, w_down):
    B, L, D = h.shape
    N = B * L
    t = h.reshape(N, D)
    cap = CAPACITY_FACTOR * N // N_EXPERTS
    aff = jax.nn.softmax(jnp.matmul(t, w_router).astype(jnp.float32), axis=-1)
    gate, idx = lax.top_k(aff.T, cap)
    xe = t[idx]
    hid = jax.nn.silu(jnp.einsum('ecd,edf->ecf', xe, w_gate)) * jnp.einsum('ecd,edf->ecf', xe, w_up)
    ye = jnp.einsum('ecf,efd->ecd', hid, w_down) * gate[..., None].astype(hid.dtype)
    out = jnp.zeros((N, D), ye.dtype).at[idx.reshape(-1)].add(ye.reshape(-1, D))
    return out.reshape(B, L, D).astype(h.dtype)


def trunk(x, meta_tokens, norm1_g, w_in, conv_w, conv_b, gate_b, pool_w, pool_scale,
          mh_norm_g, w_out, norm2_g, w_router, w_gate, w_up, w_down, final_g):
    B = x.shape[0]
    meta = jnp.broadcast_to(meta_tokens[None].astype(x.dtype), (B, N_META, D_MODEL))
    x = jnp.concatenate([meta, x], axis=1)
    o1 = D_POOL
    o2 = o1 + D_MLSTM
    o3 = o2 + D_MLSTM
    o4 = o3 + D_MLSTM
    o5 = o4 + D_MLSTM
    for l in range(DEPTH):
        h = rmsnorm(x, norm1_g[l])
        z = jnp.matmul(h, w_in[l])
        u = z[..., :o1]
        qk = jax.nn.silu(centred_dwconv(z[..., o1:o3], conv_w[l], conv_b[l]))
        q, k = qk[..., :D_MLSTM], qk[..., D_MLSTM:]
        v, o, gates = z[..., o3:o4], z[..., o4:o5], z[..., o5:]
        y_pool = pool_mixer(u, pool_w[l], pool_scale[l]).astype(x.dtype)
        y_mlstm = mlstm_mixer(q, k, v, o, gates, gate_b[l], mh_norm_g[l]).astype(x.dtype)
        x = x + jnp.matmul(jnp.concatenate([y_pool, y_mlstm], axis=-1), w_out[l])
        x = x + expert_choice_ffn(rmsnorm(x, norm2_g[l]), w_router[l], w_gate[l], w_up[l], w_down[l])
    return rmsnorm(x, final_g)[:, N_META:]


def setup_inputs(seed: int = 0) -> dict:
    key = jax.random.key(seed)
    ks = jax.random.split(key, 20)
    f32 = jnp.float32
    nrm = lambda k, s, sc: jax.random.normal(k, s, f32) * sc
    gate_base = jnp.array([0.0, 0.0, 3.0, 3.0], f32)[None, :, None]
    return {
        "x_prompt": nrm(ks[0], (BATCH, SEQ, D_MODEL), 1.0),
        "x_sample": nrm(ks[1], (DEC_BATCH, DEC_SEQ, D_MODEL), 1.0),
        "meta_tokens": nrm(ks[2], (N_META, D_MODEL), 1.0),
        "norm1_g": 1.0 + nrm(ks[3], (DEPTH, D_MODEL), 0.02),
        "w_in": nrm(ks[4], (DEPTH, D_MODEL, N_IN), D_MODEL ** -0.5),
        "conv_w": nrm(ks[5], (DEPTH, CONV_K, 2 * D_MLSTM), CONV_K ** -0.5),
        "conv_b": nrm(ks[6], (DEPTH, 2 * D_MLSTM), 0.01),
        "gate_b": gate_base + nrm(ks[7], (DEPTH, 4, N_HEADS), 0.3),
        "pool_w": nrm(ks[8], (DEPTH, N_POOL_GROUPS, D_POOL_GROUP, D_POOL_GROUP), D_POOL_GROUP ** -0.5),
        "pool_scale": 1.0 + nrm(ks[9], (DEPTH, D_POOL), 0.02),
        "mh_norm_g": 1.0 + nrm(ks[10], (DEPTH, D_MLSTM), 0.02),
        "w_out": nrm(ks[11], (DEPTH, D_MIX, D_MODEL), D_MIX ** -0.5),
        "norm2_g": 1.0 + nrm(ks[12], (DEPTH, D_MODEL), 0.02),
        "w_router": nrm(ks[13], (DEPTH, D_MODEL, N_EXPERTS), D_MODEL ** -0.5),
        "w_gate": nrm(ks[14], (DEPTH, N_EXPERTS, D_MODEL, D_FF), D_MODEL ** -0.5),
        "w_up": nrm(ks[15], (DEPTH, N_EXPERTS, D_MODEL, D_FF), D_MODEL ** -0.5),
        "w_down": nrm(ks[16], (DEPTH, N_EXPERTS, D_FF, D_MODEL), D_FF ** -0.5),
        "final_g": 1.0 + nrm(ks[17], (D_MODEL,), 0.02),
    }


def reference(x_prompt, x_sample, meta_tokens, norm1_g, w_in, conv_w, conv_b, gate_b, pool_w,
              pool_scale, mh_norm_g, w_out, norm2_g, w_router, w_gate, w_up, w_down, final_g):
    y_prompt = trunk(x_prompt, meta_tokens, norm1_g, w_in, conv_w, conv_b, gate_b, pool_w, pool_scale,
                     mh_norm_g, w_out, norm2_g, w_router, w_gate, w_up, w_down, final_g)
    y_sample = trunk(x_sample, meta_tokens, norm1_g, w_in, conv_w, conv_b, gate_b, pool_w, pool_scale,
                     mh_norm_g, w_out, norm2_g, w_router, w_gate, w_up, w_down, final_g)
    return (y_prompt, y_sample)
```

```python
import functools

import jax
import jax.numpy as jnp
from jax import lax
from jax.experimental import pallas as pl
from jax.experimental.pallas import tpu as pltpu

F32 = jnp.float32
BF16 = jnp.bfloat16
I32 = jnp.int32

D_MODEL = 1024
N_META = 16
D_POOL = 256
POOL_WINDOWS = (2, 4, 8, 16)
D_POOL_GROUP = 64
N_HEADS = 4
HEAD_DIM = 192
HEAD_PAD = 256
D_MLSTM = N_HEADS * HEAD_DIM
D_MP = N_HEADS * HEAD_PAD
N_GATE = 4 * N_HEADS
N_EXPERTS = 16
CAPACITY_FACTOR = 2
EPS = 1e-6
NEG = -1e30

LANES = 128
SUBLANES = 8
CHUNK = 128
HALO = SUBLANES
AUG = D_MODEL + LANES
ROW_TILE = 256
SLOT_TILE = 512
VMEM_LIMIT = 48 * 1024 * 1024


def _cparams(sem, vmem=VMEM_LIMIT, **kw):
    return pltpu.CompilerParams(dimension_semantics=sem, vmem_limit_bytes=vmem, **kw)


def _const_spec(shape):
    nd = len(shape)
    return pl.BlockSpec(shape, lambda *_: (0,) * nd)


def _inproj_kernel(x_ref, g_ref, wu_ref, wq_ref, wk_ref, wv_ref, wo_ref, wg_ref,
                   u_ref, q_ref, k_ref, v_ref, o_ref, gt_ref):
    x = x_ref[...]
    h = x * lax.rsqrt(jnp.mean(x * x, axis=-1, keepdims=True) + EPS) * g_ref[...]
    hb = h.astype(BF16)
    for w_ref, out_ref in ((wu_ref, u_ref), (wq_ref, q_ref), (wk_ref, k_ref),
                           (wv_ref, v_ref), (wo_ref, o_ref), (wg_ref, gt_ref)):
        out_ref[...] = jnp.dot(hb, w_ref[...], preferred_element_type=F32)


def _inproj(x, g, wu, wq, wk, wv, wo, wg):
    n = x.shape[0]
    tm = ROW_TILE
    row = lambda w: pl.BlockSpec((tm, w), lambda i: (i, 0))
    widths = (D_POOL, D_MP, D_MP, D_MP, D_MP, N_GATE)
    return pl.pallas_call(
        _inproj_kernel,
        name="inproj",
        grid=(n // tm,),
        in_specs=[row(D_MODEL), _const_spec(g.shape)] + [_const_spec(w.shape) for w in (wu, wq, wk, wv, wo, wg)],
        out_specs=[row(w) for w in widths],
        out_shape=[jax.ShapeDtypeStruct((n, w), F32) for w in widths],
        compiler_params=_cparams(("parallel",)),
    )(x, g, wu, wq, wk, wv, wo, wg)


def _silu(y):
    return y * jax.nn.sigmoid(y)


def _log_sigmoid(x):
    return jnp.minimum(x, 0.0) - jnp.log1p(jnp.exp(-jnp.abs(x)))


def _conv3(x, prev_row, next_row, w_ref, b_ref, lo, row):
    n = x.shape[0]
    xm1 = jnp.where(row == 0, prev_row, pltpu.roll(x, 1, 0))
    xp1 = jnp.where(row == n - 1, next_row, pltpu.roll(x, n - 1, 0))
    sl = slice(lo, lo + D_MP)
    return (w_ref[0:1, sl] * xm1 + w_ref[1:2, sl] * x + w_ref[2:3, sl] * xp1 + b_ref[:, sl])


def _mlstm_kernel(seq_len, n_chunks, n_batch, rev,
                  qp_ref, qprev_ref, qnext_ref, kp_ref, kprev_ref, knext_ref, v_ref, gt_ref,
                  cw_ref, cb_ref, gb_ref, h_ref, c_ref, n_ref, m_ref):
    b = pl.program_id(0)
    c = pl.program_id(1)
    cc = (n_chunks - 1 - c) if rev else c

    @pl.when(c == 0)
    def _():
        c_ref[...] = jnp.zeros_like(c_ref)
        n_ref[...] = jnp.zeros_like(n_ref)
        m_ref[...] = jnp.zeros_like(m_ref)

    row = lax.broadcasted_iota(I32, (CHUNK, 1), 0)
    valid = (cc * CHUNK + row) < seq_len
    has_prev = jnp.logical_or(b > 0, cc > 0)
    has_next = jnp.logical_or(b < n_batch - 1, cc < n_chunks - 1)

    def halo(ref, r, ok):
        return jnp.where(ok, ref[r:r + 1, :], 0.0)

    q_all = _silu(_conv3(qp_ref[...], halo(qprev_ref, HALO - 1, has_prev), halo(qnext_ref, 0, has_next),
                         cw_ref, cb_ref, 0, row))
    k_all = _silu(_conv3(kp_ref[...], halo(kprev_ref, HALO - 1, has_prev), halo(knext_ref, 0, has_next),
                         cw_ref, cb_ref, D_MP, row))
    q_all = jnp.where(valid, q_all, 0.0) * (HEAD_DIM ** -0.5)
    k_all = jnp.where(valid, k_all, 0.0)
    gates = gt_ref[...] + gb_ref[...]

    t_i = lax.broadcasted_iota(I32, (CHUNK, CHUNK), 0)
    s_i = lax.broadcasted_iota(I32, (CHUNK, CHUNK), 1)
    eye = t_i == s_i
    tri = (s_i >= t_i) if rev else (s_i <= t_i)
    tri_t = (t_i >= s_i) if rev else (t_i <= s_i)

    def to_row(col):
        return jnp.sum(jnp.where(eye, col, 0.0), axis=0, keepdims=True)

    for h in range(N_HEADS):
        sl = slice(h * HEAD_PAD, (h + 1) * HEAD_PAD)
        ji = (N_HEADS if rev else 0) + h
        jf = (3 * N_HEADS if rev else 2 * N_HEADS) + h
        li_c = jnp.where(valid, gates[:, ji:ji + 1], NEG)
        lf_c = jnp.where(valid, _log_sigmoid(gates[:, jf:jf + 1]), 0.0)
        li_r = to_row(li_c)
        lf_r = to_row(lf_c)
        b_c = jnp.sum(jnp.where(tri, lf_r, 0.0), axis=1, keepdims=True)
        b_r = jnp.sum(jnp.where(tri_t, lf_c, 0.0), axis=0, keepdims=True)
        b_tot = jnp.sum(lf_c, axis=0, keepdims=True)
        m_prev = m_ref[h]
        n_prev = n_ref[h]

        a_c = b_tot - b_c + li_c
        m_loc = jnp.max(a_c, axis=0, keepdims=True)
        wa_c = jnp.exp(a_c - m_loc)

        dlog = jnp.where(tri, b_c - b_r + li_r, -jnp.inf)
        inter = b_c + m_prev
        m_t = jnp.maximum(inter, jnp.max(dlog, axis=1, keepdims=True))
        w_inter = jnp.exp(inter - m_t)
        dexp = jnp.exp(dlog - m_t)

        q = q_all[:, sl]
        k = k_all[:, sl]
        qb = q.astype(BF16)
        kb = k.astype(BF16)
        vb = v_ref[:, sl].astype(BF16)
        scores = lax.dot_general(qb, kb, (((1,), (1,)), ((), ())), preferred_element_type=F32) * dexp
        num = (w_inter * jnp.dot(qb, c_ref[h].astype(BF16), preferred_element_type=F32)
               + jnp.dot(scores.astype(BF16), vb, preferred_element_type=F32))
        den = (w_inter * jnp.sum(q * n_prev, axis=1, keepdims=True)
               + jnp.sum(scores, axis=1, keepdims=True))
        hh = num / jnp.maximum(jnp.abs(den), jnp.exp(-m_t))
        h_ref[:, sl] = jnp.where(valid, hh, 0.0)

        m_new = jnp.maximum(b_tot + m_prev, m_loc)
        s_prev = jnp.exp(b_tot + m_prev - m_new)
        s_loc = jnp.exp(m_loc - m_new)
        kw = k * wa_c
        c_loc = lax.dot_general(kw.astype(BF16), vb, (((0,), (0,)), ((), ())), preferred_element_type=F32)
        c_ref[h] = s_prev * c_ref[h] + s_loc * c_loc
        n_ref[h] = s_prev * n_prev + s_loc * jnp.sum(kw, axis=0, keepdims=True)
        m_ref[h] = m_new


def _mlstm(qp, kp, v, gates, cw, cb, gb, n_batch, lp, seq_len, rev):
    n = qp.shape[0]
    nc = lp // CHUNK
    hpc = CHUNK // HALO
    last_halo = n // HALO - 1

    def chunk_idx(b, c):
        return b * nc + ((nc - 1 - c) if rev else c)

    main = lambda w: pl.BlockSpec((CHUNK, w), lambda b, c: (chunk_idx(b, c), 0))
    prev = pl.BlockSpec((HALO, D_MP), lambda b, c: (jnp.maximum(chunk_idx(b, c) * hpc - 1, 0), 0))
    nxt = pl.BlockSpec((HALO, D_MP), lambda b, c: (jnp.minimum((chunk_idx(b, c) + 1) * hpc, last_halo), 0))
    return pl.pallas_call(
        functools.partial(_mlstm_kernel, seq_len, nc, n_batch, rev),
        name="mlstm_bwd" if rev else "mlstm_fwd",
        grid=(n_batch, nc),
        in_specs=[main(D_MP), prev, nxt, main(D_MP), prev, nxt, main(D_MP), main(N_GATE),
                  _const_spec(cw.shape), _const_spec(cb.shape), _const_spec(gb.shape)],
        out_specs=main(D_MP),
        out_shape=jax.ShapeDtypeStruct((n, D_MP), F32),
        scratch_shapes=[pltpu.VMEM((N_HEADS, HEAD_PAD, HEAD_PAD), F32),
                        pltpu.VMEM((N_HEADS, 1, HEAD_PAD), F32),
                        pltpu.VMEM((N_HEADS, 1, 1), F32)],
        compiler_params=_cparams(("parallel", "arbitrary")),
    )(qp, qp, qp, kp, kp, kp, v, gates, cw, cb, gb)


def _positions(i, tm, lp, shape, axis):
    row0 = i * tm
    p0 = row0 - (row0 // lp) * lp
    p = p0 + lax.broadcasted_iota(I32, shape, axis)
    return jnp.where(p >= lp, p - lp, p)


def _mixout_kernel(lp, seq_len, n_tiles,
                   u_ref, uprev_ref, unext_ref, hf_ref, hb_ref, o_ref, x_ref,
                   pw_ref, ps_ref, mg_ref, wop_ref, wom_ref, g2_ref, wr_ref, wrt_ref,
                   x1_ref, aug_ref, afft_ref):
    i = pl.program_id(0)
    tm = x_ref.shape[0]
    pos = _positions(i, tm, lp, (tm, 1), 0)
    valid = pos < seq_len

    u = u_ref[...]
    ext = jnp.concatenate([jnp.where(i > 0, uprev_ref[...], 0.0), u,
                           jnp.where(i < n_tiles - 1, unext_ref[...], 0.0)], axis=0)
    n_ext = tm + 2 * HALO

    def shifted(a, d):
        return pltpu.roll(a, (-d) % n_ext, 0)

    w2 = shifted(ext, -1) + ext
    w4 = shifted(w2, -1) + shifted(w2, 1)
    w8 = shifted(w4, -2) + shifted(w4, 2)
    w16 = shifted(w8, -4) + shifted(w8, 4)
    lane = lax.broadcasted_iota(I32, (tm, D_POOL), 1)
    grp = lane // D_POOL_GROUP
    centre = slice(HALO, HALO + tm)
    wsum = jnp.where(grp == 0, w2[centre], jnp.where(grp == 1, w4[centre],
                     jnp.where(grp == 2, w8[centre], w16[centre])))
    half = jnp.where(grp == 0, 1, jnp.where(grp == 1, 2, jnp.where(grp == 2, 4, 8)))
    lo = jnp.clip(pos - half, 0, seq_len)
    hi = jnp.clip(pos + half, 0, seq_len)
    cnt = jnp.maximum(hi - lo, 1).astype(F32)
    mixed = jnp.where(valid, wsum / cnt - u, 0.0)
    y_pool = jnp.dot(mixed.astype(BF16), pw_ref[...], preferred_element_type=F32) * ps_ref[...]

    hs = hf_ref[...] + hb_ref[...]
    og = jax.nn.sigmoid(o_ref[...])
    parts = []
    for h in range(N_HEADS):
        sl = slice(h * HEAD_PAD, (h + 1) * HEAD_PAD)
        hh = hs[:, sl]
        ms = jnp.sum(hh * hh, axis=-1, keepdims=True) * (1.0 / HEAD_DIM)
        parts.append(hh * lax.rsqrt(ms + EPS))
    y_m = jnp.concatenate(parts, axis=1) * mg_ref[...] * og
    y_m = jnp.where(valid, y_m, 0.0)

    x1 = (x_ref[...] + jnp.dot(y_pool.astype(BF16), wop_ref[...], preferred_element_type=F32)
          + jnp.dot(y_m.astype(BF16), wom_ref[...], preferred_element_type=F32))
    x1_ref[...] = x1

    h2 = x1 * lax.rsqrt(jnp.mean(x1 * x1, axis=-1, keepdims=True) + EPS) * g2_ref[...]
    logits = jnp.dot(h2, wr_ref[...], preferred_element_type=F32, precision=lax.Precision.HIGHEST)
    logits = jnp.where(lax.broadcasted_iota(I32, (tm, LANES), 1) < N_EXPERTS, logits, -jnp.inf)
    e = jnp.exp(logits - jnp.max(logits, axis=-1, keepdims=True))
    aug_ref[:, :D_MODEL] = h2
    aug_ref[:, D_MODEL:] = e / jnp.sum(e, axis=-1, keepdims=True)

    logits_t = lax.dot_general(wrt_ref[...], h2, (((1,), (1,)), ((), ())),
                               preferred_element_type=F32, precision=lax.Precision.HIGHEST)
    et = jnp.exp(logits_t - jnp.max(logits_t, axis=0, keepdims=True))
    aff_t = et / jnp.sum(et, axis=0, keepdims=True)
    valid_r = _positions(i, tm, lp, (1, tm), 1) < seq_len
    afft_ref[...] = jnp.where(valid_r, aff_t, -1.0)


def _mixout(u, hf, hb, o, x, pw, ps, mg, wop, wom, g2, wr, wrt, lp, seq_len):
    n = x.shape[0]
    tm = ROW_TILE
    nt = n // tm
    hpt = tm // HALO
    row = lambda w: pl.BlockSpec((tm, w), lambda i: (i, 0))
    prev = pl.BlockSpec((HALO, D_POOL), lambda i: (jnp.maximum(i * hpt - 1, 0), 0))
    nxt = pl.BlockSpec((HALO, D_POOL), lambda i: (jnp.minimum((i + 1) * hpt, n // HALO - 1), 0))
    consts = (pw, ps, mg, wop, wom, g2, wr, wrt)
    return pl.pallas_call(
        functools.partial(_mixout_kernel, lp, seq_len, nt),
        name="mixout",
        grid=(nt,),
        in_specs=[row(D_POOL), prev, nxt, row(D_MP), row(D_MP), row(D_MP), row(D_MODEL)]
                 + [_const_spec(w.shape) for w in consts],
        out_specs=[row(D_MODEL), row(AUG), pl.BlockSpec((N_EXPERTS, tm), lambda i: (0, i))],
        out_shape=[jax.ShapeDtypeStruct((n, D_MODEL), F32), jax.ShapeDtypeStruct((n, AUG), F32),
                   jax.ShapeDtypeStruct((N_EXPERTS, n), F32)],
        compiler_params=_cparams(("parallel",)),
    )(u, u, u, hf, hb, o, x, *consts)


def _topk_kernel(cap, n_slot_tiles, aff_ref, idx_ref, mask_ref):
    ne, nb, _ = aff_ref.shape
    bits = lax.bitcast_convert_type(aff_ref[...], I32)

    def count(pred):
        return jnp.sum(jnp.sum(pred.astype(I32), axis=2, keepdims=True), axis=1, keepdims=True)

    def bisect(it, thr):
        cand = thr | (jnp.int32(1) << (30 - it))
        return jnp.where(count(bits >= cand) >= cap, cand, thr)

    thr = lax.fori_loop(0, 31, bisect, jnp.zeros((ne, 1, 1), I32))
    gt = bits > thr
    eq = bits == thr
    need = cap - count(gt)

    i_a = lax.broadcasted_iota(I32, (LANES, LANES), 0)
    i_b = lax.broadcasted_iota(I32, (LANES, LANES), 1)
    upper = (i_a <= i_b).astype(BF16)
    lower_t = (i_b <= i_a).astype(BF16)
    k_a = lax.broadcasted_iota(I32, (nb, nb), 0)
    k_b = lax.broadcasted_iota(I32, (nb, nb), 1)
    strict_lower = (k_b < k_a).astype(BF16)

    def block_prefix(local_incl):
        totals = jnp.broadcast_to(local_incl[:, LANES - 1:LANES], (nb, LANES)).astype(BF16)
        return jnp.dot(strict_lower, totals, preferred_element_type=F32)

    blk = lax.broadcasted_iota(I32, (nb, 1), 0).astype(F32)

    for e in range(ne):
        eq_e = eq[e].astype(BF16)
        eq_rank = jnp.dot(eq_e, upper, preferred_element_type=F32)
        eq_rank = eq_rank + block_prefix(eq_rank)
        need_e = need[e].astype(F32)
        sel = jnp.logical_or(gt[e], jnp.logical_and(eq[e], eq_rank <= need_e))
        mask_ref[e] = sel.astype(BF16)

    def compact(e, carry):
        m_e = mask_ref[e]
        local = jnp.dot(m_e, upper, preferred_element_type=F32)
        local_t = lax.dot_general(lower_t, m_e, (((1,), (1,)), ((), ())),
                                  preferred_element_type=F32).astype(BF16)
        bp_excl = block_prefix(local)[:, 0:1]
        bp_incl = bp_excl + local[:, LANES - 1:LANES]
        for jt in range(n_slot_tiles):
            j = (jt * SLOT_TILE + lax.broadcasted_iota(I32, (1, SLOT_TILE), 1)).astype(F32)
            onehot = jnp.logical_and(bp_excl <= j, j < bp_incl)
            oh = onehot.astype(F32)
            base = jnp.sum(oh * bp_excl, axis=0, keepdims=True)
            blk_j = jnp.sum(oh * blk, axis=0, keepdims=True)
            cum_t = jnp.dot(local_t, onehot.astype(BF16), preferred_element_type=F32)
            in_blk = jnp.sum((cum_t <= (j - base)).astype(F32), axis=0, keepdims=True)
            idx_ref[e, :, jt * SLOT_TILE:(jt + 1) * SLOT_TILE] = (blk_j * LANES + in_blk).astype(I32)
        return carry

    lax.fori_loop(0, ne, compact, 0)


def _topk(aff_t, cap):
    ne, n = aff_t.shape
    nb = n // LANES
    nbp = -(-nb // LANES) * LANES
    aff3 = jnp.pad(aff_t.reshape(ne, nb, LANES), ((0, 0), (0, nbp - nb), (0, 0)), constant_values=-1.0)
    nst = -(-cap // SLOT_TILE)
    idx = pl.pallas_call(
        functools.partial(_topk_kernel, cap, nst),
        name="topk",
        grid=(1,),
        in_specs=[_const_spec(aff3.shape)],
        out_specs=_const_spec((ne, 1, nst * SLOT_TILE)),
        out_shape=jax.ShapeDtypeStruct((ne, 1, nst * SLOT_TILE), I32),
        scratch_shapes=[pltpu.VMEM((ne, nbp, LANES), BF16)],
        compiler_params=_cparams(("arbitrary",)),
    )(aff3)
    return idx[:, 0, :cap]


def _ffn_kernel(rows, n_tiles, idx_ref, aug_hbm, x_hbm, wg_ref, wu_ref, wd_ref, out_hbm,
                xbuf, obuf, sem_x, sem_g, sem_s):
    e = pl.program_id(0)
    i = pl.program_id(1)
    step = e * n_tiles + i
    slot = step % 2
    n_steps = pl.num_programs(0) * n_tiles

    def row_copies(s, sl, kind):
        base = s * rows

        def body(r, carry):
            tok = idx_ref[base + r]
            if kind == "x":
                pltpu.make_async_copy(aug_hbm.at[pl.ds(tok, 1)], xbuf.at[sl, pl.ds(r, 1)], sem_x.at[sl]).start()
            elif kind == "g":
                pltpu.make_async_copy(out_hbm.at[pl.ds(tok, 1)], obuf.at[sl, pl.ds(r, 1)], sem_g.at[sl]).start()
            else:
                pltpu.make_async_copy(obuf.at[sl, pl.ds(r, 1)], out_hbm.at[pl.ds(tok, 1)], sem_s.at[sl]).start()
            return carry

        lax.fori_loop(0, rows, body, 0)

    def wait_all(kind, sl):
        if kind == "x":
            pltpu.make_async_copy(aug_hbm.at[pl.ds(0, rows)], xbuf.at[sl], sem_x.at[sl]).wait()
        elif kind == "g":
            pltpu.make_async_copy(out_hbm.at[pl.ds(0, rows)], obuf.at[sl], sem_g.at[sl]).wait()
        else:
            pltpu.make_async_copy(obuf.at[sl], out_hbm.at[pl.ds(0, rows)], sem_s.at[sl]).wait()

    @pl.when(step == 0)
    def _():
        row_copies(step, slot, "x")

    @pl.when(step > 0)
    def _():
        wait_all("s", 1 - slot)

    @pl.when(i == 0)
    def _():
        row_copies(step, slot, "g")

    @pl.when(step + 1 < n_steps)
    def _():
        row_copies(step + 1, 1 - slot, "x")

    @pl.when(i + 1 < n_tiles)
    def _():
        row_copies(step + 1, 1 - slot, "g")

    wait_all("x", slot)
    xa = xbuf[slot]
    xb = xa[:, :D_MODEL].astype(BF16)
    lane = lax.broadcasted_iota(I32, (rows, LANES), 1)
    gate = jnp.sum(jnp.where(lane == e, xa[:, D_MODEL:], 0.0), axis=1, keepdims=True)
    hid = (_silu(jnp.dot(xb, wg_ref[0], preferred_element_type=F32))
           * jnp.dot(xb, wu_ref[0], preferred_element_type=F32))
    ye = jnp.dot(hid.astype(BF16), wd_ref[0], preferred_element_type=F32) * gate
    wait_all("g", slot)
    obuf[slot] = obuf[slot] + ye
    row_copies(step, slot, "s")

    @pl.when(step == n_steps - 1)
    def _():
        wait_all("s", slot)


def _ffn(idx, aug, x1, wg, wu, wd):
    ne, cap = idx.shape
    rows = max(r for r in range(SUBLANES, 513, SUBLANES) if cap % r == 0)
    nt = cap // rows
    wspec = pl.BlockSpec((1, D_MODEL, D_MODEL), lambda e, i, idx_ref: (e, 0, 0))
    any_spec = pl.BlockSpec(memory_space=pl.ANY)
    grid_spec = pltpu.PrefetchScalarGridSpec(
        num_scalar_prefetch=1,
        grid=(ne, nt),
        in_specs=[any_spec, any_spec, wspec, wspec, wspec],
        out_specs=any_spec,
        scratch_shapes=[pltpu.VMEM((2, rows, AUG), F32), pltpu.VMEM((2, rows, D_MODEL), F32),
                        pltpu.SemaphoreType.DMA((2,)), pltpu.SemaphoreType.DMA((2,)),
                        pltpu.SemaphoreType.DMA((2,))],
    )
    return pl.pallas_call(
        functools.partial(_ffn_kernel, rows, nt),
        name="expert_ffn",
        grid_spec=grid_spec,
        out_shape=jax.ShapeDtypeStruct(x1.shape, F32),
        input_output_aliases={2: 0},
        compiler_params=_cparams(("arbitrary", "arbitrary"), has_side_effects=True),
    )(idx.reshape(-1), aug, x1, wg, wu, wd)


def _final_kernel(x_ref, g_ref, y_ref):
    x = x_ref[...]
    y_ref[...] = x * lax.rsqrt(jnp.mean(x * x, axis=-1, keepdims=True) + EPS) * g_ref[...]


def _final_norm(x, g):
    n = x.shape[0]
    tm = ROW_TILE
    spec = pl.BlockSpec((tm, D_MODEL), lambda i: (i, 0))
    return pl.pallas_call(
        _final_kernel, name="final_norm", grid=(n // tm,), in_specs=[spec, _const_spec(g.shape)], out_specs=spec,
        out_shape=jax.ShapeDtypeStruct(x.shape, F32), compiler_params=_cparams(("parallel",)),
    )(x, g)


def _pad_heads(w, axis):
    shape = w.shape
    w = w.reshape(shape[:axis] + (N_HEADS, HEAD_DIM) + shape[axis + 1:])
    pad = [(0, 0)] * w.ndim
    pad[axis + 1] = (0, HEAD_PAD - HEAD_DIM)
    return jnp.pad(w, pad).reshape(shape[:axis] + (D_MP,) + shape[axis + 1:])


def _prepare(norm1_g, w_in, conv_w, conv_b, gate_b, pool_w, pool_scale, mh_norm_g, w_out, norm2_g,
             w_router, w_gate, w_up, w_down, final_g):
    depth = w_in.shape[0]
    o1 = D_POOL
    offs = [o1 + j * D_MLSTM for j in range(5)]
    layers = []
    for l in range(depth):
        w = w_in[l]
        proj = [_pad_heads(w[:, offs[j]:offs[j + 1]], 1).astype(BF16) for j in range(4)]
        eye = jnp.eye(len(POOL_WINDOWS), dtype=F32)
        pw = (eye[:, None, :, None] * pool_w[l][:, :, None, :]).reshape(D_POOL, D_POOL)
        layers.append(dict(
            g1=norm1_g[l][None], wu=w[:, :o1].astype(BF16), wq=proj[0], wk=proj[1], wv=proj[2], wo=proj[3],
            wgate=w[:, offs[4]:].astype(BF16),
            cw=jnp.concatenate([_pad_heads(conv_w[l][:, :D_MLSTM], 1), _pad_heads(conv_w[l][:, D_MLSTM:], 1)], axis=1),
            cb=jnp.concatenate([_pad_heads(conv_b[l][:D_MLSTM], 0), _pad_heads(conv_b[l][D_MLSTM:], 0)])[None],
            gb=gate_b[l].reshape(1, N_GATE),
            pw=pw.astype(BF16), ps=pool_scale[l][None], mg=_pad_heads(mh_norm_g[l], 0)[None],
            wop=w_out[l][:D_POOL].astype(BF16), wom=_pad_heads(w_out[l][D_POOL:], 0).astype(BF16),
            g2=norm2_g[l][None], wr=jnp.pad(w_router[l], ((0, 0), (0, LANES - N_EXPERTS))), wrt=w_router[l].T,
            eg=w_gate[l].astype(BF16), eu=w_up[l].astype(BF16), ed=w_down[l].astype(BF16),
        ))
    return layers, final_g[None]


def _trunk(x_in, meta_tokens, layers, final_g):
    n_batch, s, _ = x_in.shape
    seq_len = s + N_META
    lp = -(-(seq_len + HALO) // CHUNK) * CHUNK
    n_tok = n_batch * seq_len
    cap = CAPACITY_FACTOR * n_tok // N_EXPERTS
    meta = jnp.broadcast_to(meta_tokens[None].astype(x_in.dtype), (n_batch, N_META, D_MODEL))
    x = jnp.concatenate([meta, x_in, jnp.zeros((n_batch, lp - seq_len, D_MODEL), x_in.dtype)], axis=1)
    x = x.reshape(n_batch * lp, D_MODEL)
    for p in layers:
        u, qp, kp, v, o, gates = _inproj(x, p["g1"], p["wu"], p["wq"], p["wk"], p["wv"], p["wo"], p["wgate"])
        hf = _mlstm(qp, kp, v, gates, p["cw"], p["cb"], p["gb"], n_batch, lp, seq_len, False)
        hb = _mlstm(qp, kp, v, gates, p["cw"], p["cb"], p["gb"], n_batch, lp, seq_len, True)
        x1, aug, aff_t = _mixout(u, hf, hb, o, x, p["pw"], p["ps"], p["mg"], p["wop"], p["wom"],
                                 p["g2"], p["wr"], p["wrt"], lp, seq_len)
        idx = _topk(aff_t, cap)
        x = _ffn(idx, aug, x1, p["eg"], p["eu"], p["ed"])
    y = _final_norm(x, final_g)
    return y.reshape(n_batch, lp, D_MODEL)[:, N_META:seq_len]


def kernel(x_prompt, x_sample, meta_tokens, norm1_g, w_in, conv_w, conv_b, gate_b, pool_w, pool_scale,
           mh_norm_g, w_out, norm2_g, w_router, w_gate, w_up, w_down, final_g):
    layers, fg = _prepare(norm1_g, w_in, conv_w, conv_b, gate_b, pool_w, pool_scale, mh_norm_g, w_out,
                          norm2_g, w_router, w_gate, w_up, w_down, final_g)
    return (_trunk(x_prompt, meta_tokens, layers, fg), _trunk(x_sample, meta_tokens, layers, fg))
```

```python
import functools

import jax
import jax.numpy as jnp
from jax import lax
from jax.experimental import pallas as pl
from jax.experimental.pallas import tpu as pltpu

F32 = jnp.float32
BF16 = jnp.bfloat16
I32 = jnp.int32

D_MODEL = 1024
N_META = 16
D_POOL = 256
POOL_WINDOWS = (2, 4, 8, 16)
D_POOL_GROUP = 64
N_HEADS = 4
HEAD_DIM = 192
HEAD_PAD = 256
D_MLSTM = N_HEADS * HEAD_DIM
D_MP = N_HEADS * HEAD_PAD
N_GATE = 4 * N_HEADS
N_EXPERTS = 16
CAPACITY_FACTOR = 2
EPS = 1e-6
NEG = -1e30

LANES = 128
SUBLANES = 8
CHUNK = 128
HALO = SUBLANES
AUG = D_MODEL + LANES
ROW_TILE = 256
AFF_PARTS = 3
DISPATCH_TILE = 256
SLAB = 64
VMEM_LIMIT = 48 * 1024 * 1024


def _cparams(sem, vmem=VMEM_LIMIT, **kw):
    return pltpu.CompilerParams(dimension_semantics=sem, vmem_limit_bytes=vmem, **kw)


def _const_spec(shape):
    nd = len(shape)
    return pl.BlockSpec(shape, lambda *_: (0,) * nd)


def _inproj_kernel(x_ref, g_ref, wu_ref, wq_ref, wk_ref, wv_ref, wo_ref, wg_ref,
                   u_ref, q_ref, k_ref, v_ref, o_ref, gt_ref):
    x = x_ref[...]
    h = x * lax.rsqrt(jnp.mean(x * x, axis=-1, keepdims=True) + EPS) * g_ref[...]
    hb = h.astype(BF16)
    for w_ref, out_ref in ((wu_ref, u_ref), (wq_ref, q_ref), (wk_ref, k_ref),
                           (wv_ref, v_ref), (wo_ref, o_ref), (wg_ref, gt_ref)):
        out_ref[...] = jnp.dot(hb, w_ref[...], preferred_element_type=F32)


def _inproj(x, g, wu, wq, wk, wv, wo, wg):
    n = x.shape[0]
    tm = ROW_TILE
    row = lambda w: pl.BlockSpec((tm, w), lambda i: (i, 0))
    widths = (D_POOL, D_MP, D_MP, D_MP, D_MP, N_GATE)
    return pl.pallas_call(
        _inproj_kernel,
        name="inproj",
        grid=(n // tm,),
        in_specs=[row(D_MODEL), _const_spec(g.shape)] + [_const_spec(w.shape) for w in (wu, wq, wk, wv, wo, wg)],
        out_specs=[row(w) for w in widths],
        out_shape=[jax.ShapeDtypeStruct((n, w), F32) for w in widths],
        compiler_params=_cparams(("parallel",)),
    )(x, g, wu, wq, wk, wv, wo, wg)


def _silu(y):
    return y * jax.nn.sigmoid(y)


def _log_sigmoid(x):
    return jnp.minimum(x, 0.0) - jnp.log1p(jnp.exp(-jnp.abs(x)))


def _conv3(x, prev_row, next_row, w_ref, b_ref, lo, row):
    n = x.shape[0]
    xm1 = jnp.where(row == 0, prev_row, pltpu.roll(x, 1, 0))
    xp1 = jnp.where(row == n - 1, next_row, pltpu.roll(x, n - 1, 0))
    sl = slice(lo, lo + D_MP)
    return (w_ref[0:1, sl] * xm1 + w_ref[1:2, sl] * x + w_ref[2:3, sl] * xp1 + b_ref[:, sl])


def _mlstm_kernel(seq_len, n_chunks, n_batch, rev,
                  qp_ref, qprev_ref, qnext_ref, kp_ref, kprev_ref, knext_ref, v_ref, gt_ref,
                  cw_ref, cb_ref, gb_ref, h_ref, c_ref, n_ref, m_ref):
    b = pl.program_id(0)
    c = pl.program_id(1)
    cc = (n_chunks - 1 - c) if rev else c

    @pl.when(c == 0)
    def _():
        c_ref[...] = jnp.zeros_like(c_ref)
        n_ref[...] = jnp.zeros_like(n_ref)
        m_ref[...] = jnp.zeros_like(m_ref)

    row = lax.broadcasted_iota(I32, (CHUNK, 1), 0)
    valid = (cc * CHUNK + row) < seq_len
    has_prev = jnp.logical_or(b > 0, cc > 0)
    has_next = jnp.logical_or(b < n_batch - 1, cc < n_chunks - 1)

    def halo(ref, r, ok):
        return jnp.where(ok, ref[r:r + 1, :], 0.0)

    q_all = _silu(_conv3(qp_ref[...], halo(qprev_ref, HALO - 1, has_prev), halo(qnext_ref, 0, has_next),
                         cw_ref, cb_ref, 0, row))
    k_all = _silu(_conv3(kp_ref[...], halo(kprev_ref, HALO - 1, has_prev), halo(knext_ref, 0, has_next),
                         cw_ref, cb_ref, D_MP, row))
    q_all = jnp.where(valid, q_all, 0.0) * (HEAD_DIM ** -0.5)
    k_all = jnp.where(valid, k_all, 0.0)
    gates = gt_ref[...] + gb_ref[...]

    t_i = lax.broadcasted_iota(I32, (CHUNK, CHUNK), 0)
    s_i = lax.broadcasted_iota(I32, (CHUNK, CHUNK), 1)
    eye = t_i == s_i
    tri = (s_i >= t_i) if rev else (s_i <= t_i)
    tri_t = (t_i >= s_i) if rev else (t_i <= s_i)

    def to_row(col):
        return jnp.sum(jnp.where(eye, col, 0.0), axis=0, keepdims=True)

    for h in range(N_HEADS):
        sl = slice(h * HEAD_PAD, (h + 1) * HEAD_PAD)
        ji = (N_HEADS if rev else 0) + h
        jf = (3 * N_HEADS if rev else 2 * N_HEADS) + h
        li_c = jnp.where(valid, gates[:, ji:ji + 1], NEG)
        lf_c = jnp.where(valid, _log_sigmoid(gates[:, jf:jf + 1]), 0.0)
        li_r = to_row(li_c)
        lf_r = to_row(lf_c)
        b_c = jnp.sum(jnp.where(tri, lf_r, 0.0), axis=1, keepdims=True)
        b_r = jnp.sum(jnp.where(tri_t, lf_c, 0.0), axis=0, keepdims=True)
        b_tot = jnp.sum(lf_c, axis=0, keepdims=True)
        m_prev = m_ref[h]
        n_prev = n_ref[h]

        a_c = b_tot - b_c + li_c
        m_loc = jnp.max(a_c, axis=0, keepdims=True)
        wa_c = jnp.exp(a_c - m_loc)

        dlog = jnp.where(tri, b_c - b_r + li_r, -jnp.inf)
        inter = b_c + m_prev
        m_t = jnp.maximum(inter, jnp.max(dlog, axis=1, keepdims=True))
        w_inter = jnp.exp(inter - m_t)
        dexp = jnp.exp(dlog - m_t)

        q = q_all[:, sl]
        k = k_all[:, sl]
        qb = q.astype(BF16)
        kb = k.astype(BF16)
        vb = v_ref[:, sl].astype(BF16)
        scores = lax.dot_general(qb, kb, (((1,), (1,)), ((), ())), preferred_element_type=F32) * dexp
        num = (w_inter * jnp.dot(qb, c_ref[h].astype(BF16), preferred_element_type=F32)
               + jnp.dot(scores.astype(BF16), vb, preferred_element_type=F32))
        den = (w_inter * jnp.sum(q * n_prev, axis=1, keepdims=True)
               + jnp.sum(scores, axis=1, keepdims=True))
        hh = num / jnp.maximum(jnp.abs(den), jnp.exp(-m_t))
        h_ref[:, sl] = jnp.where(valid, hh, 0.0)

        m_new = jnp.maximum(b_tot + m_prev, m_loc)
        s_prev = jnp.exp(b_tot + m_prev - m_new)
        s_loc = jnp.exp(m_loc - m_new)
        kw = k * wa_c
        c_loc = lax.dot_general(kw.astype(BF16), vb, (((0,), (0,)), ((), ())), preferred_element_type=F32)
        c_ref[h] = s_prev * c_ref[h] + s_loc * c_loc
        n_ref[h] = s_prev * n_prev + s_loc * jnp.sum(kw, axis=0, keepdims=True)
        m_ref[h] = m_new


def _mlstm(qp, kp, v, gates, cw, cb, gb, n_batch, lp, seq_len, rev):
    n = qp.shape[0]
    nc = lp // CHUNK
    hpc = CHUNK // HALO
    last_halo = n // HALO - 1

    def chunk_idx(b, c):
        return b * nc + ((nc - 1 - c) if rev else c)

    main = lambda w: pl.BlockSpec((CHUNK, w), lambda b, c: (chunk_idx(b, c), 0))
    prev = pl.BlockSpec((HALO, D_MP), lambda b, c: (jnp.maximum(chunk_idx(b, c) * hpc - 1, 0), 0))
    nxt = pl.BlockSpec((HALO, D_MP), lambda b, c: (jnp.minimum((chunk_idx(b, c) + 1) * hpc, last_halo), 0))
    return pl.pallas_call(
        functools.partial(_mlstm_kernel, seq_len, nc, n_batch, rev),
        name="mlstm_bwd" if rev else "mlstm_fwd",
        grid=(n_batch, nc),
        in_specs=[main(D_MP), prev, nxt, main(D_MP), prev, nxt, main(D_MP), main(N_GATE),
                  _const_spec(cw.shape), _const_spec(cb.shape), _const_spec(gb.shape)],
        out_specs=main(D_MP),
        out_shape=jax.ShapeDtypeStruct((n, D_MP), F32),
        scratch_shapes=[pltpu.VMEM((N_HEADS, HEAD_PAD, HEAD_PAD), F32),
                        pltpu.VMEM((N_HEADS, 1, HEAD_PAD), F32),
                        pltpu.VMEM((N_HEADS, 1, 1), F32)],
        compiler_params=_cparams(("parallel", "arbitrary")),
    )(qp, qp, qp, kp, kp, kp, v, gates, cw, cb, gb)


def _positions(i, tm, lp, shape, axis):
    row0 = i * tm
    p0 = row0 - (row0 // lp) * lp
    p = p0 + lax.broadcasted_iota(I32, shape, axis)
    return jnp.where(p >= lp, p - lp, p)


def _mixout_kernel(lp, seq_len, n_tiles,
                   u_ref, uprev_ref, unext_ref, hf_ref, hb_ref, o_ref, x_ref,
                   pw_ref, ps_ref, mg_ref, wop_ref, wom_ref, g2_ref, wr_ref, wrt_ref,
                   x1_ref, aug_ref, afft_ref):
    i = pl.program_id(0)
    tm = x_ref.shape[0]
    pos = _positions(i, tm, lp, (tm, 1), 0)
    valid = pos < seq_len

    u = u_ref[...]
    ext = jnp.concatenate([jnp.where(i > 0, uprev_ref[...], 0.0), u,
                           jnp.where(i < n_tiles - 1, unext_ref[...], 0.0)], axis=0)
    n_ext = tm + 2 * HALO

    def shifted(a, d):
        return pltpu.roll(a, (-d) % n_ext, 0)

    w2 = shifted(ext, -1) + ext
    w4 = shifted(w2, -1) + shifted(w2, 1)
    w8 = shifted(w4, -2) + shifted(w4, 2)
    w16 = shifted(w8, -4) + shifted(w8, 4)
    lane = lax.broadcasted_iota(I32, (tm, D_POOL), 1)
    grp = lane // D_POOL_GROUP
    centre = slice(HALO, HALO + tm)
    wsum = jnp.where(grp == 0, w2[centre], jnp.where(grp == 1, w4[centre],
                     jnp.where(grp == 2, w8[centre], w16[centre])))
    half = jnp.where(grp == 0, 1, jnp.where(grp == 1, 2, jnp.where(grp == 2, 4, 8)))
    lo = jnp.clip(pos - half, 0, seq_len)
    hi = jnp.clip(pos + half, 0, seq_len)
    cnt = jnp.maximum(hi - lo, 1).astype(F32)
    mixed = jnp.where(valid, wsum / cnt - u, 0.0)
    y_pool = jnp.dot(mixed.astype(BF16), pw_ref[...], preferred_element_type=F32) * ps_ref[...]

    hs = hf_ref[...] + hb_ref[...]
    og = jax.nn.sigmoid(o_ref[...])
    parts = []
    for h in range(N_HEADS):
        sl = slice(h * HEAD_PAD, (h + 1) * HEAD_PAD)
        hh = hs[:, sl]
        ms = jnp.sum(hh * hh, axis=-1, keepdims=True) * (1.0 / HEAD_DIM)
        parts.append(hh * lax.rsqrt(ms + EPS))
    y_m = jnp.concatenate(parts, axis=1) * mg_ref[...] * og
    y_m = jnp.where(valid, y_m, 0.0)

    x1 = (x_ref[...] + jnp.dot(y_pool.astype(BF16), wop_ref[...], preferred_element_type=F32)
          + jnp.dot(y_m.astype(BF16), wom_ref[...], preferred_element_type=F32))
    x1_ref[...] = x1

    h2 = x1 * lax.rsqrt(jnp.mean(x1 * x1, axis=-1, keepdims=True) + EPS) * g2_ref[...]
    logits = jnp.dot(h2, wr_ref[...], preferred_element_type=F32, precision=lax.Precision.HIGHEST)
    logits = jnp.where(lax.broadcasted_iota(I32, (tm, LANES), 1) < N_EXPERTS, logits, -jnp.inf)
    e = jnp.exp(logits - jnp.max(logits, axis=-1, keepdims=True))
    aff = e / jnp.sum(e, axis=-1, keepdims=True)
    hi = aff.astype(BF16).astype(F32)
    mid = (aff - hi).astype(BF16).astype(F32)
    lo = aff - hi - mid
    aug_ref[:, :D_MODEL] = h2.astype(BF16)
    aug_ref[:, D_MODEL:] = (hi + pltpu.roll(mid, N_EXPERTS, 1) + pltpu.roll(lo, 2 * N_EXPERTS, 1)).astype(BF16)

    logits_t = lax.dot_general(wrt_ref[...], h2, (((1,), (1,)), ((), ())),
                               preferred_element_type=F32, precision=lax.Precision.HIGHEST)
    et = jnp.exp(logits_t - jnp.max(logits_t, axis=0, keepdims=True))
    aff_t = et / jnp.sum(et, axis=0, keepdims=True)
    valid_r = _positions(i, tm, lp, (1, tm), 1) < seq_len
    afft_ref[...] = jnp.where(valid_r, aff_t, -1.0)


def _mixout(u, hf, hb, o, x, pw, ps, mg, wop, wom, g2, wr, wrt, lp, seq_len):
    n = x.shape[0]
    tm = ROW_TILE
    nt = n // tm
    hpt = tm // HALO
    row = lambda w: pl.BlockSpec((tm, w), lambda i: (i, 0))
    prev = pl.BlockSpec((HALO, D_POOL), lambda i: (jnp.maximum(i * hpt - 1, 0), 0))
    nxt = pl.BlockSpec((HALO, D_POOL), lambda i: (jnp.minimum((i + 1) * hpt, n // HALO - 1), 0))
    consts = (pw, ps, mg, wop, wom, g2, wr, wrt)
    return pl.pallas_call(
        functools.partial(_mixout_kernel, lp, seq_len, nt),
        name="mixout",
        grid=(nt,),
        in_specs=[row(D_POOL), prev, nxt, row(D_MP), row(D_MP), row(D_MP), row(D_MODEL)]
                 + [_const_spec(w.shape) for w in consts],
        out_specs=[row(D_MODEL), row(AUG), pl.BlockSpec((N_EXPERTS, tm), lambda i: (0, i))],
        out_shape=[jax.ShapeDtypeStruct((n, D_MODEL), F32), jax.ShapeDtypeStruct((n, AUG), BF16),
                   jax.ShapeDtypeStruct((N_EXPERTS, n), F32)],
        compiler_params=_cparams(("parallel",)),
    )(u, u, u, hf, hb, o, x, *consts)


def _topk_kernel(cap, aff_ref, mask_ref, bpx_ref):
    ne, nb, _ = aff_ref.shape
    bits = lax.bitcast_convert_type(aff_ref[...], I32)

    def count(pred):
        return jnp.sum(jnp.sum(pred.astype(I32), axis=2, keepdims=True), axis=1, keepdims=True)

    def bisect(it, thr):
        cand = thr | (jnp.int32(1) << (30 - it))
        return jnp.where(count(bits >= cand) >= cap, cand, thr)

    thr = lax.fori_loop(0, 31, bisect, jnp.zeros((ne, 1, 1), I32))
    gt = bits > thr
    eq = bits == thr
    need = cap - count(gt)

    i_a = lax.broadcasted_iota(I32, (LANES, LANES), 0)
    i_b = lax.broadcasted_iota(I32, (LANES, LANES), 1)
    upper = (i_a <= i_b).astype(BF16)
    k_a = lax.broadcasted_iota(I32, (nb, nb), 0)
    k_b = lax.broadcasted_iota(I32, (nb, nb), 1)
    strict_lower = (k_b < k_a).astype(BF16)

    def block_prefix(local_incl):
        totals = jnp.broadcast_to(local_incl[:, LANES - 1:LANES], (nb, LANES)).astype(BF16)
        return jnp.dot(strict_lower, totals, preferred_element_type=F32)

    for e in range(ne):
        eq_e = eq[e].astype(BF16)
        eq_rank = jnp.dot(eq_e, upper, preferred_element_type=F32)
        eq_rank = eq_rank + block_prefix(eq_rank)
        need_e = need[e].astype(F32)
        sel = jnp.logical_or(gt[e], jnp.logical_and(eq[e], eq_rank <= need_e)).astype(BF16)
        mask_ref[e] = sel
        bpx_ref[e] = block_prefix(jnp.dot(sel, upper, preferred_element_type=F32)).astype(I32)


def _topk(aff_t, cap):
    ne, n = aff_t.shape
    nb = n // LANES
    nbp = -(-nb // LANES) * LANES
    aff3 = jnp.pad(aff_t.reshape(ne, nb, LANES), ((0, 0), (0, nbp - nb), (0, 0)), constant_values=-1.0)
    mask, bpx = pl.pallas_call(
        functools.partial(_topk_kernel, cap),
        name="topk",
        grid=(1,),
        in_specs=[_const_spec(aff3.shape)],
        out_specs=[_const_spec(aff3.shape), _const_spec(aff3.shape)],
        out_shape=[jax.ShapeDtypeStruct(aff3.shape, BF16), jax.ShapeDtypeStruct(aff3.shape, I32)],
        compiler_params=_cparams(("arbitrary",)),
    )(aff3)
    starts = bpx[:, 0:nb:DISPATCH_TILE // LANES, 0]
    counts = jnp.concatenate([starts[:, 1:], jnp.full((ne, 1), cap, I32)], axis=1) - starts
    return mask.reshape(ne, nbp * LANES), starts.T.reshape(-1), counts.T.reshape(-1)


def _tile_ranks(m):
    t = m.shape[1]
    before = (lax.broadcasted_iota(I32, (t, t), 0) < lax.broadcasted_iota(I32, (t, t), 1)).astype(BF16)
    rank = jnp.dot(m, before, preferred_element_type=F32)
    return jnp.where(m > 0, rank, -1.0)


def _f(x):
    return jnp.asarray(x, I32).astype(F32)


def _rank_rows(r, n_rows, off, lo=0, hi=None):
    w = lax.broadcasted_iota(I32, (n_rows, r.shape[1]), 0).astype(F32)
    hit = jnp.logical_and(r == w + _f(off), r >= _f(lo))
    if hi is not None:
        hit = jnp.logical_and(hit, r < _f(hi))
    return hit


def _smax(xs):
    most = xs[0]
    for x in xs[1:]:
        most = jnp.maximum(most, x)
    return most


def _dispatch_kernel(stride, pad, n_tiles, a_ref, c_ref, aug_ref, mask_ref, xe_hbm, xc, xo, carry, sem, sem_o):
    win = SLAB + SUBLANES
    per = win + SUBLANES
    i = pl.program_id(0)
    slot = i % 2
    start = [a_ref[i * N_EXPERTS + e] for e in range(N_EXPERTS)]
    count = [c_ref[i * N_EXPERTS + e] for e in range(N_EXPERTS)]
    head = [a & (SUBLANES - 1) for a in start]
    base = [a - h for a, h in zip(start, head)]
    end = [a + c for a, c in zip(start, count)]
    head2 = [x & (SUBLANES - 1) for x in end]
    base2 = [x - h for x, h in zip(end, head2)]
    rankm = _tile_ranks(mask_ref[...])
    aug = aug_ref[...]

    @pl.when(i == 0)
    def _():
        carry[...] = jnp.zeros_like(carry)
        xo[...] = jnp.zeros_like(xo)
        pads = [pltpu.make_async_copy(xo.at[pl.ds(0, pad)], xe_hbm.at[pl.ds(e * stride + stride - pad, pad)], sem_o)
                for e in range(N_EXPERTS)]
        for cp in pads:
            cp.start()
        for cp in pads:
            cp.wait()

    parts = []
    for e in range(N_EXPERTS):
        r = rankm[e:e + 1, :]
        sub = lax.broadcasted_iota(I32, (SUBLANES, r.shape[1]), 0)
        nxt = jnp.logical_and(_rank_rows(r, SUBLANES, base2[e] - start[e]), sub < head2[e])
        parts += [_rank_rows(r, win, -head[e]).astype(F32), nxt.astype(F32)]
    xc[slot] = jnp.dot(jnp.concatenate(parts, axis=0).astype(BF16), aug, preferred_element_type=F32)
    for e in range(N_EXPERTS):
        old = carry[e * SUBLANES:(e + 1) * SUBLANES, :]
        first = pl.ds(e * per, SUBLANES)
        xc[slot, first, :] = xc[slot, first, :] + old
        carry[e * SUBLANES:(e + 1) * SUBLANES, :] = (xc[slot, pl.ds(e * per + win, SUBLANES), :]
                                                      + jnp.where(base2[e] == base[e], old, 0.0))

    def wait_slot(sl):
        pltpu.make_async_copy(xc.at[sl, pl.ds(0, N_EXPERTS * win)], xe_hbm.at[pl.ds(0, N_EXPERTS * win)],
                              sem.at[sl]).wait()

    @pl.when(i > 0)
    def _():
        wait_slot(1 - slot)

    for e in range(N_EXPERTS):
        dst = e * stride + pl.multiple_of(base[e], SUBLANES)
        pltpu.make_async_copy(xc.at[slot, pl.ds(e * per, win)], xe_hbm.at[pl.ds(dst, win)], sem.at[slot]).start()

    def extra_copy(e, k):
        dst = e * stride + pl.multiple_of(base[e] + win + SLAB * (k - 1), SUBLANES)
        return pltpu.make_async_copy(xo.at[pl.ds(e * SLAB, SLAB)], xe_hbm.at[pl.ds(dst, SLAB)], sem_o)

    def extra_window(k, _):
        p = [_rank_rows(rankm[e:e + 1, :], SLAB, win - head[e] + SLAB * (k - 1)).astype(F32)
             for e in range(N_EXPERTS)]
        xo[...] = jnp.dot(jnp.concatenate(p, axis=0).astype(BF16), aug, preferred_element_type=F32)
        for e in range(N_EXPERTS):
            @pl.when(count[e] + head[e] > win + SLAB * (k - 1))
            def _():
                extra_copy(e, k).start()
        for e in range(N_EXPERTS):
            @pl.when(count[e] + head[e] > win + SLAB * (k - 1))
            def _():
                extra_copy(e, k).wait()
        return 0

    most = _smax([c + h for c, h in zip(count, head)])
    n_extra = lax.div(jnp.maximum(most - win, 0) + (SLAB - 1), jnp.int32(SLAB))
    lax.fori_loop(1, n_extra + 1, extra_window, 0)

    @pl.when(i == n_tiles - 1)
    def _():
        wait_slot(slot)


def _dispatch(starts, counts, aug, mask, stride, pad):
    assert pad <= N_EXPERTS * SLAB
    n = aug.shape[0]
    t = DISPATCH_TILE
    nt = n // t
    per = SLAB + 2 * SUBLANES
    grid_spec = pltpu.PrefetchScalarGridSpec(
        num_scalar_prefetch=2,
        grid=(nt,),
        in_specs=[pl.BlockSpec((t, AUG), lambda i, a, c: (i, 0)),
                  pl.BlockSpec((N_EXPERTS, t), lambda i, a, c: (0, i))],
        out_specs=pl.BlockSpec(memory_space=pl.ANY),
        scratch_shapes=[pltpu.VMEM((2, N_EXPERTS * per, AUG), F32), pltpu.VMEM((N_EXPERTS * SLAB, AUG), F32),
                        pltpu.VMEM((N_EXPERTS * SUBLANES, AUG), F32),
                        pltpu.SemaphoreType.DMA((2,)), pltpu.SemaphoreType.DMA(())],
    )
    return pl.pallas_call(
        functools.partial(_dispatch_kernel, stride, pad, nt),
        name="dispatch",
        grid_spec=grid_spec,
        out_shape=jax.ShapeDtypeStruct((N_EXPERTS * stride, AUG), F32),
        compiler_params=_cparams(("arbitrary",), has_side_effects=True),
    )(starts, counts, aug, mask)


def _ffn_kernel(xe_ref, wg_ref, wu_ref, wd_ref, ye_ref):
    e = pl.program_id(0)
    xa = xe_ref[...]
    xb = xa[:, :D_MODEL].astype(BF16)
    lane = lax.broadcasted_iota(I32, (xa.shape[0], LANES), 1)
    mine = jnp.logical_and((lane & (N_EXPERTS - 1)) == e, lane < AFF_PARTS * N_EXPERTS)
    gate = jnp.sum(jnp.where(mine, xa[:, D_MODEL:], 0.0), axis=1, keepdims=True)
    hid = (_silu(jnp.dot(xb, wg_ref[0], preferred_element_type=F32))
           * jnp.dot(xb, wu_ref[0], preferred_element_type=F32))
    ye_ref[...] = jnp.dot(hid.astype(BF16), wd_ref[0], preferred_element_type=F32) * gate


def _ffn(xe, wg, wu, wd, cap, rows, stride):
    spe = stride // rows
    npe = cap // rows
    wspec = pl.BlockSpec((1, D_MODEL, D_MODEL), lambda e, i: (e, 0, 0))
    return pl.pallas_call(
        _ffn_kernel,
        name="expert_ffn",
        grid=(N_EXPERTS, npe),
        in_specs=[pl.BlockSpec((rows, AUG), lambda e, i: (e * spe + i, 0)), wspec, wspec, wspec],
        out_specs=pl.BlockSpec((rows, D_MODEL), lambda e, i: (e * npe + i, 0)),
        out_shape=jax.ShapeDtypeStruct((N_EXPERTS * cap, D_MODEL), F32),
        compiler_params=_cparams(("parallel", "arbitrary")),
    )(xe, wg, wu, wd)


def _combine_kernel(cap, stride, n_tiles, a_ref, c_ref, x1_ref, mask_ref, ye_hbm, out_ref, yc, yo, sem, sem_o):
    win = SLAB + SUBLANES
    i = pl.program_id(0)
    slot = i % 2

    def window_start(tile, e, k):
        first = a_ref[tile * N_EXPERTS + e] + SLAB * k
        return pl.multiple_of(jnp.minimum(first - (first & (SUBLANES - 1)), cap - win), SUBLANES)

    def window_copy(tile, e, k, buf, s):
        return pltpu.make_async_copy(ye_hbm.at[pl.ds(e * stride + window_start(tile, e, k), win)],
                                     buf.at[pl.ds(e * win, win)], s)

    def wait_all(buf, s):
        pltpu.make_async_copy(ye_hbm.at[pl.ds(0, N_EXPERTS * win)], buf, s).wait()

    @pl.when(i == 0)
    def _():
        for e in range(N_EXPERTS):
            window_copy(i, e, 0, yc.at[slot], sem.at[slot]).start()

    @pl.when(i + 1 < n_tiles)
    def _():
        for e in range(N_EXPERTS):
            window_copy(i + 1, e, 0, yc.at[1 - slot], sem.at[1 - slot]).start()

    starts = [a_ref[i * N_EXPERTS + e] for e in range(N_EXPERTS)]
    counts = [c_ref[i * N_EXPERTS + e] for e in range(N_EXPERTS)]
    rankm = _tile_ranks(mask_ref[...])

    def scattered(k, y):
        p = jnp.concatenate(
            [_rank_rows(rankm[e:e + 1, :], win, window_start(i, e, k) - starts[e], SLAB * k, SLAB * (k + 1))
             .astype(F32) for e in range(N_EXPERTS)], axis=0).astype(BF16)
        y_hi = y.astype(BF16)
        y_lo = (y - y_hi.astype(F32)).astype(BF16)
        tn = (((0,), (0,)), ((), ()))
        return (lax.dot_general(p, y_hi, tn, preferred_element_type=F32)
                + lax.dot_general(p, y_lo, tn, preferred_element_type=F32))

    wait_all(yc.at[slot], sem.at[slot])
    out_ref[...] = x1_ref[...] + scattered(0, yc[slot])

    def extra_window(k, carry):
        for e in range(N_EXPERTS):
            window_copy(i, e, k, yo, sem_o).start()
        wait_all(yo, sem_o)
        out_ref[...] += scattered(k, yo[...])
        return carry

    lax.fori_loop(1, lax.div(_smax(counts) + (SLAB - 1), jnp.int32(SLAB)), extra_window, 0)


def _combine(starts, counts, x1, mask, ye, cap, stride):
    n = x1.shape[0]
    t = DISPATCH_TILE
    nt = n // t
    win = SLAB + SUBLANES
    row = pl.BlockSpec((t, D_MODEL), lambda i, a, c: (i, 0))
    grid_spec = pltpu.PrefetchScalarGridSpec(
        num_scalar_prefetch=2,
        grid=(nt,),
        in_specs=[row, pl.BlockSpec((N_EXPERTS, t), lambda i, a, c: (0, i)), pl.BlockSpec(memory_space=pl.ANY)],
        out_specs=row,
        scratch_shapes=[pltpu.VMEM((2, N_EXPERTS * win, D_MODEL), F32), pltpu.VMEM((N_EXPERTS * win, D_MODEL), F32),
                        pltpu.SemaphoreType.DMA((2,)), pltpu.SemaphoreType.DMA(())],
    )
    return pl.pallas_call(
        functools.partial(_combine_kernel, cap, stride, nt),
        name="combine",
        grid_spec=grid_spec,
        out_shape=jax.ShapeDtypeStruct(x1.shape, F32),
        compiler_params=_cparams(("arbitrary",)),
    )(starts, counts, x1, mask, ye)


def _moe(x1, aug, aff_t, wg, wu, wd, cap):
    rows = max(r for r in range(SLAB + SUBLANES, 513, SUBLANES) if cap % r == 0)
    stride = cap + rows
    mask, starts, counts = _topk(aff_t, cap)
    xe = _dispatch(starts, counts, aug, mask, stride, rows)
    ye = _ffn(xe, wg, wu, wd, cap, rows, stride)
    return _combine(starts, counts, x1, mask, ye, cap, cap)


def _final_kernel(x_ref, g_ref, y_ref):
    x = x_ref[...]
    y_ref[...] = x * lax.rsqrt(jnp.mean(x * x, axis=-1, keepdims=True) + EPS) * g_ref[...]


def _final_norm(x, g):
    n = x.shape[0]
    tm = ROW_TILE
    spec = pl.BlockSpec((tm, D_MODEL), lambda i: (i, 0))
    return pl.pallas_call(
        _final_kernel, name="final_norm", grid=(n // tm,), in_specs=[spec, _const_spec(g.shape)], out_specs=spec,
        out_shape=jax.ShapeDtypeStruct(x.shape, F32), compiler_params=_cparams(("parallel",)),
    )(x, g)


def _pad_heads(w, axis):
    shape = w.shape
    w = w.reshape(shape[:axis] + (N_HEADS, HEAD_DIM) + shape[axis + 1:])
    pad = [(0, 0)] * w.ndim
    pad[axis + 1] = (0, HEAD_PAD - HEAD_DIM)
    return jnp.pad(w, pad).reshape(shape[:axis] + (D_MP,) + shape[axis + 1:])


def _prepare(norm1_g, w_in, conv_w, conv_b, gate_b, pool_w, pool_scale, mh_norm_g, w_out, norm2_g,
             w_router, w_gate, w_up, w_down, final_g):
    depth = w_in.shape[0]
    o1 = D_POOL
    offs = [o1 + j * D_MLSTM for j in range(5)]
    layers = []
    for l in range(depth):
        w = w_in[l]
        proj = [_pad_heads(w[:, offs[j]:offs[j + 1]], 1).astype(BF16) for j in range(4)]
        eye = jnp.eye(len(POOL_WINDOWS), dtype=F32)
        pw = (eye[:, None, :, None] * pool_w[l][:, :, None, :]).reshape(D_POOL, D_POOL)
        layers.append(dict(
            g1=norm1_g[l][None], wu=w[:, :o1].astype(BF16), wq=proj[0], wk=proj[1], wv=proj[2], wo=proj[3],
            wgate=w[:, offs[4]:].astype(BF16),
            cw=jnp.concatenate([_pad_heads(conv_w[l][:, :D_MLSTM], 1), _pad_heads(conv_w[l][:, D_MLSTM:], 1)], axis=1),
            cb=jnp.concatenate([_pad_heads(conv_b[l][:D_MLSTM], 0), _pad_heads(conv_b[l][D_MLSTM:], 0)])[None],
            gb=gate_b[l].reshape(1, N_GATE),
            pw=pw.astype(BF16), ps=pool_scale[l][None], mg=_pad_heads(mh_norm_g[l], 0)[None],
            wop=w_out[l][:D_POOL].astype(BF16), wom=_pad_heads(w_out[l][D_POOL:], 0).astype(BF16),
            g2=norm2_g[l][None], wr=jnp.pad(w_router[l], ((0, 0), (0, LANES - N_EXPERTS))), wrt=w_router[l].T,
            eg=w_gate[l].astype(BF16), eu=w_up[l].astype(BF16), ed=w_down[l].astype(BF16),
        ))
    return layers, final_g[None]


def _trunk(x_in, meta_tokens, layers, final_g):
    n_batch, s, _ = x_in.shape
    seq_len = s + N_META
    lp = -(-(seq_len + HALO) // CHUNK) * CHUNK
    n_tok = n_batch * seq_len
    cap = CAPACITY_FACTOR * n_tok // N_EXPERTS
    meta = jnp.broadcast_to(meta_tokens[None].astype(x_in.dtype), (n_batch, N_META, D_MODEL))
    x = jnp.concatenate([meta, x_in, jnp.zeros((n_batch, lp - seq_len, D_MODEL), x_in.dtype)], axis=1)
    x = x.reshape(n_batch * lp, D_MODEL)
    for p in layers:
        u, qp, kp, v, o, gates = _inproj(x, p["g1"], p["wu"], p["wq"], p["wk"], p["wv"], p["wo"], p["wgate"])
        hf = _mlstm(qp, kp, v, gates, p["cw"], p["cb"], p["gb"], n_batch, lp, seq_len, False)
        hb = _mlstm(qp, kp, v, gates, p["cw"], p["cb"], p["gb"], n_batch, lp, seq_len, True)
        x1, aug, aff_t = _mixout(u, hf, hb, o, x, p["pw"], p["ps"], p["mg"], p["wop"], p["wom"],
                                 p["g2"], p["wr"], p["wrt"], lp, seq_len)
        x = _moe(x1, aug, aff_t, p["eg"], p["eu"], p["ed"], cap)
    y = _final_norm(x, final_g)
    return y.reshape(n_batch, lp, D_MODEL)[:, N_META:seq_len]


def kernel(x_prompt, x_sample, meta_tokens, norm1_g, w_in, conv_w, conv_b, gate_b, pool_w, pool_scale,
           mh_norm_g, w_out, norm2_g, w_router, w_gate, w_up, w_down, final_g):
    layers, fg = _prepare(norm1_g, w_in, conv_w, conv_b, gate_b, pool_w, pool_scale, mh_norm_g, w_out,
                          norm2_g, w_router, w_gate, w_up, w_down, final_g)
    return (_trunk(x_prompt, meta_tokens, layers, fg), _trunk(x_sample, meta_tokens, layers, fg))
```

```python
import functools

import jax
import jax.numpy as jnp
from jax import lax
from jax.experimental import pallas as pl
from jax.experimental.pallas import tpu as pltpu

F32 = jnp.float32
BF16 = jnp.bfloat16
I32 = jnp.int32

D_MODEL = 1024
N_META = 16
D_POOL = 256
POOL_WINDOWS = (2, 4, 8, 16)
D_POOL_GROUP = 64
N_HEADS = 4
HEAD_DIM = 192
HEAD_PAD = 256
D_MLSTM = N_HEADS * HEAD_DIM
D_MP = N_HEADS * HEAD_PAD
N_GATE = 4 * N_HEADS
N_EXPERTS = 16
CAPACITY_FACTOR = 2
EPS = 1e-6
NEG = -1e30

LANES = 128
SUBLANES = 8
CHUNK = 128
HALO = SUBLANES
AUG = D_MODEL + LANES
ROW_TILE = 256
AFF_PARTS = 3
DISPATCH_TILE = 256
SLAB = 64
VMEM_LIMIT = 48 * 1024 * 1024


def _cparams(sem, vmem=VMEM_LIMIT, **kw):
    return pltpu.CompilerParams(dimension_semantics=sem, vmem_limit_bytes=vmem, **kw)


def _const_spec(shape):
    nd = len(shape)
    return pl.BlockSpec(shape, lambda *_: (0,) * nd)


def _silu(y):
    return y * jax.nn.sigmoid(y)


def _inproj_kernel(lp, seq_len, n_tiles, x_ref, xprev_ref, xnext_ref, g_ref, wu_ref, wqk_ref, wv_ref, wo_ref,
                   wg_ref, cw_ref, cb_ref, u_ref, q_ref, k_ref, v_ref, o_ref, gt_ref):
    i = pl.program_id(0)
    tm = x_ref.shape[0]
    x = jnp.concatenate([jnp.where(i > 0, xprev_ref[...], 0.0), x_ref[...],
                         jnp.where(i < n_tiles - 1, xnext_ref[...], 0.0)], axis=0)
    h = x * lax.rsqrt(jnp.mean(x * x, axis=-1, keepdims=True) + EPS) * g_ref[...]
    n_ext = tm + 2 * HALO
    z = jnp.dot(h.astype(BF16), wqk_ref[...], preferred_element_type=F32)
    y = (cw_ref[0:1, :] * pltpu.roll(z, 1, 0) + cw_ref[1:2, :] * z
         + cw_ref[2:3, :] * pltpu.roll(z, n_ext - 1, 0) + cb_ref[...])[HALO:HALO + tm]
    valid = _positions(i, tm, lp, (tm, 1), 0) < seq_len
    y = jnp.where(valid, _silu(y), 0.0)
    q_ref[...] = (y[:, :D_MP] * (HEAD_DIM ** -0.5)).astype(BF16)
    k_ref[...] = y[:, D_MP:].astype(BF16)
    hc = h[HALO:HALO + tm].astype(BF16)
    for w_ref, out_ref in ((wu_ref, u_ref), (wv_ref, v_ref), (wo_ref, o_ref), (wg_ref, gt_ref)):
        out_ref[...] = jnp.dot(hc, w_ref[...], preferred_element_type=F32).astype(out_ref.dtype)


def _inproj(x, g, wu, wqk, wv, wo, wg, cw, cb, lp, seq_len):
    n = x.shape[0]
    tm = ROW_TILE
    nt = n // tm
    hpt = tm // HALO
    row = lambda w: pl.BlockSpec((tm, w), lambda i: (i, 0))
    prev = pl.BlockSpec((HALO, D_MODEL), lambda i: (jnp.maximum(i * hpt - 1, 0), 0))
    nxt = pl.BlockSpec((HALO, D_MODEL), lambda i: (jnp.minimum((i + 1) * hpt, n // HALO - 1), 0))
    consts = (g, wu, wqk, wv, wo, wg, cw, cb)
    outs = ((D_POOL, F32), (D_MP, BF16), (D_MP, BF16), (D_MP, BF16), (D_MP, F32), (N_GATE, F32))
    return pl.pallas_call(
        functools.partial(_inproj_kernel, lp, seq_len, nt),
        name="inproj",
        grid=(nt,),
        in_specs=[row(D_MODEL), prev, nxt] + [_const_spec(w.shape) for w in consts],
        out_specs=[row(w) for w, _ in outs],
        out_shape=[jax.ShapeDtypeStruct((n, w), dt) for w, dt in outs],
        compiler_params=_cparams(("parallel",)),
    )(x, x, x, *consts)


def _log_sigmoid(x):
    return jnp.minimum(x, 0.0) - jnp.log1p(jnp.exp(-jnp.abs(x)))


def _mlstm_chunk(rev, valid, q_ref, k_ref, v_ref, gates, h_ref, c_ref, n_ref, m_ref):
    t_i = lax.broadcasted_iota(I32, (CHUNK, CHUNK), 0)
    s_i = lax.broadcasted_iota(I32, (CHUNK, CHUNK), 1)
    eye = t_i == s_i
    tri = (s_i >= t_i) if rev else (s_i <= t_i)
    tri_t = (t_i >= s_i) if rev else (t_i <= s_i)

    def to_row(col):
        return jnp.sum(jnp.where(eye, col, 0.0), axis=0, keepdims=True)

    log_i = jnp.where(valid, gates, NEG)
    log_f = jnp.where(valid, _log_sigmoid(gates), 0.0)
    for h in range(N_HEADS):
        sl = slice(h * HEAD_PAD, (h + 1) * HEAD_PAD)
        st = (N_HEADS if rev else 0) + h
        ji = (N_HEADS if rev else 0) + h
        jf = (3 * N_HEADS if rev else 2 * N_HEADS) + h
        li_c = log_i[:, ji:ji + 1]
        lf_c = log_f[:, jf:jf + 1]
        li_r = to_row(li_c)
        lf_r = to_row(lf_c)
        b_c = jnp.sum(jnp.where(tri, lf_r, 0.0), axis=1, keepdims=True)
        b_r = jnp.sum(jnp.where(tri_t, lf_c, 0.0), axis=0, keepdims=True)
        b_tot = jnp.sum(lf_c, axis=0, keepdims=True)
        m_prev = m_ref[st]
        n_prev = n_ref[st]

        a_c = b_tot - b_c + li_c
        m_loc = jnp.max(a_c, axis=0, keepdims=True)
        wa_c = jnp.exp(a_c - m_loc)

        dlog = jnp.where(tri, b_c - b_r + li_r, -jnp.inf)
        inter = b_c + m_prev
        m_t = jnp.maximum(inter, jnp.max(dlog, axis=1, keepdims=True))
        w_inter = jnp.exp(inter - m_t)
        dexp = jnp.exp(dlog - m_t)

        qb = q_ref[:, sl]
        kb = k_ref[:, sl]
        vb = v_ref[:, sl]
        scores = lax.dot_general(qb, kb, (((1,), (1,)), ((), ())), preferred_element_type=F32) * dexp
        num = (w_inter * jnp.dot(qb, c_ref[st].astype(BF16), preferred_element_type=F32)
               + jnp.dot(scores.astype(BF16), vb, preferred_element_type=F32))
        den = (w_inter * jnp.sum(qb.astype(F32) * n_prev, axis=1, keepdims=True)
               + jnp.sum(scores, axis=1, keepdims=True))
        hh = num / jnp.maximum(jnp.abs(den), jnp.exp(-m_t))
        h_ref[:, sl] = jnp.where(valid, hh, 0.0)

        m_new = jnp.maximum(b_tot + m_prev, m_loc)
        s_prev = jnp.exp(b_tot + m_prev - m_new)
        s_loc = jnp.exp(m_loc - m_new)
        kw = kb.astype(F32) * wa_c
        c_loc = lax.dot_general(kw.astype(BF16), vb, (((0,), (0,)), ((), ())), preferred_element_type=F32)
        c_ref[st] = s_prev * c_ref[st] + s_loc * c_loc
        n_ref[st] = s_prev * n_prev + s_loc * jnp.sum(kw, axis=0, keepdims=True)
        m_ref[st] = m_new


def _mlstm_kernel(seq_len, n_chunks, qf_ref, kf_ref, vf_ref, gf_ref, qb_ref, kb_ref, vb_ref, gb_ref, bias_ref,
                  hf_ref, hb_ref, c_ref, n_ref, m_ref):
    c = pl.program_id(1)

    @pl.when(c == 0)
    def _():
        c_ref[...] = jnp.zeros_like(c_ref)
        n_ref[...] = jnp.zeros_like(n_ref)
        m_ref[...] = jnp.zeros_like(m_ref)

    row = lax.broadcasted_iota(I32, (CHUNK, 1), 0)
    _mlstm_chunk(False, (c * CHUNK + row) < seq_len, qf_ref, kf_ref, vf_ref, gf_ref[...] + bias_ref[...],
                 hf_ref, c_ref, n_ref, m_ref)
    _mlstm_chunk(True, ((n_chunks - 1 - c) * CHUNK + row) < seq_len, qb_ref, kb_ref, vb_ref,
                 gb_ref[...] + bias_ref[...], hb_ref, c_ref, n_ref, m_ref)


def _mlstm(q, k, v, gates, bias, n_batch, lp, seq_len):
    n = q.shape[0]
    nc = lp // CHUNK
    fwd = lambda w: pl.BlockSpec((CHUNK, w), lambda b, c: (b * nc + c, 0))
    bwd = lambda w: pl.BlockSpec((CHUNK, w), lambda b, c: (b * nc + nc - 1 - c, 0))
    return pl.pallas_call(
        functools.partial(_mlstm_kernel, seq_len, nc),
        name="mlstm",
        grid=(n_batch, nc),
        in_specs=[fwd(D_MP), fwd(D_MP), fwd(D_MP), fwd(N_GATE), bwd(D_MP), bwd(D_MP), bwd(D_MP), bwd(N_GATE),
                  _const_spec(bias.shape)],
        out_specs=[fwd(D_MP), bwd(D_MP)],
        out_shape=[jax.ShapeDtypeStruct((n, D_MP), F32), jax.ShapeDtypeStruct((n, D_MP), F32)],
        scratch_shapes=[pltpu.VMEM((2 * N_HEADS, HEAD_PAD, HEAD_PAD), F32),
                        pltpu.VMEM((2 * N_HEADS, 1, HEAD_PAD), F32),
                        pltpu.VMEM((2 * N_HEADS, 1, 1), F32)],
        compiler_params=_cparams(("parallel", "arbitrary")),
    )(q, k, v, gates, q, k, v, gates, bias)


def _positions(i, tm, lp, shape, axis):
    row0 = i * tm
    p0 = row0 - (row0 // lp) * lp
    p = p0 + lax.broadcasted_iota(I32, shape, axis)
    return jnp.where(p >= lp, p - lp, p)


def _mixout_kernel(lp, seq_len, n_tiles,
                   u_ref, uprev_ref, unext_ref, hf_ref, hb_ref, o_ref, x_ref,
                   pw_ref, ps_ref, mg_ref, wop_ref, wom_ref, g2_ref, wrh_ref, wrl_ref,
                   x1_ref, aug_ref, afft_ref):
    i = pl.program_id(0)
    tm = x_ref.shape[0]
    pos = _positions(i, tm, lp, (tm, 1), 0)
    valid = pos < seq_len

    u = u_ref[...]
    ext = jnp.concatenate([jnp.where(i > 0, uprev_ref[...], 0.0), u,
                           jnp.where(i < n_tiles - 1, unext_ref[...], 0.0)], axis=0)
    n_ext = tm + 2 * HALO

    def shifted(a, d):
        return pltpu.roll(a, (-d) % n_ext, 0)

    w2 = shifted(ext, -1) + ext
    w4 = shifted(w2, -1) + shifted(w2, 1)
    w8 = shifted(w4, -2) + shifted(w4, 2)
    w16 = shifted(w8, -4) + shifted(w8, 4)
    lane = lax.broadcasted_iota(I32, (tm, D_POOL), 1)
    grp = lane // D_POOL_GROUP
    centre = slice(HALO, HALO + tm)
    wsum = jnp.where(grp == 0, w2[centre], jnp.where(grp == 1, w4[centre],
                     jnp.where(grp == 2, w8[centre], w16[centre])))
    half = jnp.where(grp == 0, 1, jnp.where(grp == 1, 2, jnp.where(grp == 2, 4, 8)))
    lo = jnp.clip(pos - half, 0, seq_len)
    hi = jnp.clip(pos + half, 0, seq_len)
    cnt = jnp.maximum(hi - lo, 1).astype(F32)
    mixed = jnp.where(valid, wsum / cnt - u, 0.0)
    y_pool = jnp.dot(mixed.astype(BF16), pw_ref[...], preferred_element_type=F32) * ps_ref[...]

    hs = hf_ref[...] + hb_ref[...]
    og = jax.nn.sigmoid(o_ref[...])
    parts = []
    for h in range(N_HEADS):
        sl = slice(h * HEAD_PAD, (h + 1) * HEAD_PAD)
        hh = hs[:, sl]
        ms = jnp.sum(hh * hh, axis=-1, keepdims=True) * (1.0 / HEAD_DIM)
        parts.append(hh * lax.rsqrt(ms + EPS))
    y_m = jnp.concatenate(parts, axis=1) * mg_ref[...] * og
    y_m = jnp.where(valid, y_m, 0.0)

    x1 = (x_ref[...] + jnp.dot(y_pool.astype(BF16), wop_ref[...], preferred_element_type=F32)
          + jnp.dot(y_m.astype(BF16), wom_ref[...], preferred_element_type=F32))
    x1_ref[...] = x1

    h2 = x1 * lax.rsqrt(jnp.mean(x1 * x1, axis=-1, keepdims=True) + EPS) * g2_ref[...]
    h_hi = h2.astype(BF16)
    h_lo = (h2 - h_hi.astype(F32)).astype(BF16)
    logits = (jnp.dot(h_hi, wrh_ref[...], preferred_element_type=F32)
              + jnp.dot(h_lo, wrh_ref[...], preferred_element_type=F32)
              + jnp.dot(h_hi, wrl_ref[...], preferred_element_type=F32))
    lane = lax.broadcasted_iota(I32, (tm, LANES), 1)
    logits = jnp.where(lane < N_EXPERTS, logits, -jnp.inf)
    e = jnp.exp(logits - jnp.max(logits, axis=-1, keepdims=True))
    aff = e / jnp.sum(e, axis=-1, keepdims=True)
    hi = aff.astype(BF16).astype(F32)
    mid = (aff - hi).astype(BF16).astype(F32)
    lo = aff - hi - mid
    parts = (hi + pltpu.roll(mid, N_EXPERTS, 1) + pltpu.roll(lo, 2 * N_EXPERTS, 1)).astype(BF16)
    aug_ref[:, :D_MODEL] = h_hi
    aug_ref[:, D_MODEL:] = parts

    e_i = lax.broadcasted_iota(I32, (N_EXPERTS, LANES), 0)
    l_i = lax.broadcasted_iota(I32, (N_EXPERTS, LANES), 1)
    pick = jnp.logical_and((l_i & (N_EXPERTS - 1)) == e_i, l_i < AFF_PARTS * N_EXPERTS).astype(BF16)
    aff_t = lax.dot_general(pick, parts, (((1,), (1,)), ((), ())), preferred_element_type=F32)
    valid_r = _positions(i, tm, lp, (1, tm), 1) < seq_len
    afft_ref[...] = jnp.where(valid_r, aff_t, -1.0)


def _mixout(u, hf, hb, o, x, pw, ps, mg, wop, wom, g2, wrh, wrl, lp, seq_len):
    n = x.shape[0]
    tm = ROW_TILE
    nt = n // tm
    hpt = tm // HALO
    row = lambda w: pl.BlockSpec((tm, w), lambda i: (i, 0))
    prev = pl.BlockSpec((HALO, D_POOL), lambda i: (jnp.maximum(i * hpt - 1, 0), 0))
    nxt = pl.BlockSpec((HALO, D_POOL), lambda i: (jnp.minimum((i + 1) * hpt, n // HALO - 1), 0))
    consts = (pw, ps, mg, wop, wom, g2, wrh, wrl)
    return pl.pallas_call(
        functools.partial(_mixout_kernel, lp, seq_len, nt),
        name="mixout",
        grid=(nt,),
        in_specs=[row(D_POOL), prev, nxt, row(D_MP), row(D_MP), row(D_MP), row(D_MODEL)]
                 + [_const_spec(w.shape) for w in consts],
        out_specs=[row(D_MODEL), row(AUG), pl.BlockSpec((N_EXPERTS, tm), lambda i: (0, i))],
        out_shape=[jax.ShapeDtypeStruct((n, D_MODEL), F32), jax.ShapeDtypeStruct((n, AUG), BF16),
                   jax.ShapeDtypeStruct((N_EXPERTS, n), F32)],
        compiler_params=_cparams(("parallel",)),
    )(u, u, u, hf, hb, o, x, *consts)


def _topk_kernel(cap, aff_ref, mask_ref, bpx_ref):
    ne, nb, _ = aff_ref.shape
    bits = lax.bitcast_convert_type(aff_ref[...], I32)

    def count(pred):
        return jnp.sum(jnp.sum(pred.astype(I32), axis=2, keepdims=True), axis=1, keepdims=True)

    def bisect(it, thr):
        cand = thr | (jnp.int32(1) << (30 - it))
        return jnp.where(count(bits >= cand) >= cap, cand, thr)

    thr = lax.fori_loop(0, 31, bisect, jnp.zeros((ne, 1, 1), I32))
    gt = bits > thr
    eq = bits == thr
    need = cap - count(gt)

    i_a = lax.broadcasted_iota(I32, (LANES, LANES), 0)
    i_b = lax.broadcasted_iota(I32, (LANES, LANES), 1)
    upper = (i_a <= i_b).astype(BF16)
    k_a = lax.broadcasted_iota(I32, (nb, nb), 0)
    k_b = lax.broadcasted_iota(I32, (nb, nb), 1)
    strict_lower = (k_b < k_a).astype(BF16)

    def block_prefix(local_incl):
        totals = jnp.broadcast_to(local_incl[:, LANES - 1:LANES], (nb, LANES)).astype(BF16)
        return jnp.dot(strict_lower, totals, preferred_element_type=F32)

    for e in range(ne):
        eq_e = eq[e].astype(BF16)
        eq_rank = jnp.dot(eq_e, upper, preferred_element_type=F32)
        eq_rank = eq_rank + block_prefix(eq_rank)
        need_e = need[e].astype(F32)
        sel = jnp.logical_or(gt[e], jnp.logical_and(eq[e], eq_rank <= need_e)).astype(BF16)
        mask_ref[e] = sel
        bpx_ref[e] = block_prefix(jnp.dot(sel, upper, preferred_element_type=F32)).astype(I32)


def _topk(aff_t, cap):
    ne, n = aff_t.shape
    nb = n // LANES
    nbp = -(-nb // LANES) * LANES
    aff3 = jnp.pad(aff_t.reshape(ne, nb, LANES), ((0, 0), (0, nbp - nb), (0, 0)), constant_values=-1.0)
    mask, bpx = pl.pallas_call(
        functools.partial(_topk_kernel, cap),
        name="topk",
        grid=(1,),
        in_specs=[_const_spec(aff3.shape)],
        out_specs=[_const_spec(aff3.shape), _const_spec(aff3.shape)],
        out_shape=[jax.ShapeDtypeStruct(aff3.shape, BF16), jax.ShapeDtypeStruct(aff3.shape, I32)],
        compiler_params=_cparams(("arbitrary",)),
    )(aff3)
    starts = bpx[:, 0:nb:DISPATCH_TILE // LANES, 0]
    counts = jnp.concatenate([starts[:, 1:], jnp.full((ne, 1), cap, I32)], axis=1) - starts
    return mask.reshape(ne, nbp * LANES), starts.T.reshape(-1), counts.T.reshape(-1)


def _tile_ranks(m):
    t = m.shape[1]
    before = (lax.broadcasted_iota(I32, (t, t), 0) < lax.broadcasted_iota(I32, (t, t), 1)).astype(BF16)
    rank = jnp.dot(m, before, preferred_element_type=F32)
    return jnp.where(m > 0, rank, -1.0)


def _f(x):
    return jnp.asarray(x, I32).astype(F32)


def _rank_rows(r, n_rows, off, lo=0, hi=None):
    w = lax.broadcasted_iota(I32, (n_rows, r.shape[1]), 0).astype(F32)
    hit = jnp.logical_and(r == w + _f(off), r >= _f(lo))
    if hi is not None:
        hit = jnp.logical_and(hit, r < _f(hi))
    return hit


def _smax(xs):
    most = xs[0]
    for x in xs[1:]:
        most = jnp.maximum(most, x)
    return most


def _dispatch_kernel(stride, pad, n_tiles, a_ref, c_ref, aug_ref, mask_ref, xe_hbm, xc, xo, carry, sem, sem_o):
    win = SLAB + SUBLANES
    per = win + SUBLANES
    i = pl.program_id(0)
    slot = i % 2
    start = [a_ref[i * N_EXPERTS + e] for e in range(N_EXPERTS)]
    count = [c_ref[i * N_EXPERTS + e] for e in range(N_EXPERTS)]
    head = [a & (SUBLANES - 1) for a in start]
    base = [a - h for a, h in zip(start, head)]
    end = [a + c for a, c in zip(start, count)]
    head2 = [x & (SUBLANES - 1) for x in end]
    base2 = [x - h for x, h in zip(end, head2)]
    rankm = _tile_ranks(mask_ref[...])
    aug = aug_ref[...]

    @pl.when(i == 0)
    def _():
        carry[...] = jnp.zeros_like(carry)
        xo[...] = jnp.zeros_like(xo)
        pads = [pltpu.make_async_copy(xo.at[pl.ds(0, pad)], xe_hbm.at[pl.ds(e * stride + stride - pad, pad)], sem_o)
                for e in range(N_EXPERTS)]
        for cp in pads:
            cp.start()
        for cp in pads:
            cp.wait()

    parts = []
    for e in range(N_EXPERTS):
        r = rankm[e:e + 1, :]
        sub = lax.broadcasted_iota(I32, (SUBLANES, r.shape[1]), 0)
        nxt = jnp.logical_and(_rank_rows(r, SUBLANES, base2[e] - start[e]), sub < head2[e])
        parts += [_rank_rows(r, win, -head[e]).astype(F32), nxt.astype(F32)]
    xc[slot] = jnp.dot(jnp.concatenate(parts, axis=0).astype(BF16), aug, preferred_element_type=F32)
    for e in range(N_EXPERTS):
        old = carry[e * SUBLANES:(e + 1) * SUBLANES, :]
        first = pl.ds(e * per, SUBLANES)
        xc[slot, first, :] = xc[slot, first, :] + old
        carry[e * SUBLANES:(e + 1) * SUBLANES, :] = (xc[slot, pl.ds(e * per + win, SUBLANES), :]
                                                      + jnp.where(base2[e] == base[e], old, 0.0))

    def wait_slot(sl):
        pltpu.make_async_copy(xc.at[sl, pl.ds(0, N_EXPERTS * win)], xe_hbm.at[pl.ds(0, N_EXPERTS * win)],
                              sem.at[sl]).wait()

    @pl.when(i > 0)
    def _():
        wait_slot(1 - slot)

    for e in range(N_EXPERTS):
        dst = e * stride + pl.multiple_of(base[e], SUBLANES)
        pltpu.make_async_copy(xc.at[slot, pl.ds(e * per, win)], xe_hbm.at[pl.ds(dst, win)], sem.at[slot]).start()

    def extra_copy(e, k):
        dst = e * stride + pl.multiple_of(base[e] + win + SLAB * (k - 1), SUBLANES)
        return pltpu.make_async_copy(xo.at[pl.ds(e * SLAB, SLAB)], xe_hbm.at[pl.ds(dst, SLAB)], sem_o)

    def extra_window(k, _):
        p = [_rank_rows(rankm[e:e + 1, :], SLAB, win - head[e] + SLAB * (k - 1)).astype(F32)
             for e in range(N_EXPERTS)]
        xo[...] = jnp.dot(jnp.concatenate(p, axis=0).astype(BF16), aug, preferred_element_type=F32)
        for e in range(N_EXPERTS):
            @pl.when(count[e] + head[e] > win + SLAB * (k - 1))
            def _():
                extra_copy(e, k).start()
        for e in range(N_EXPERTS):
            @pl.when(count[e] + head[e] > win + SLAB * (k - 1))
            def _():
                extra_copy(e, k).wait()
        return 0

    most = _smax([c + h for c, h in zip(count, head)])
    n_extra = lax.div(jnp.maximum(most - win, 0) + (SLAB - 1), jnp.int32(SLAB))
    lax.fori_loop(1, n_extra + 1, extra_window, 0)

    @pl.when(i == n_tiles - 1)
    def _():
        wait_slot(slot)


def _dispatch(starts, counts, aug, mask, stride, pad):
    assert pad <= N_EXPERTS * SLAB
    n = aug.shape[0]
    t = DISPATCH_TILE
    nt = n // t
    per = SLAB + 2 * SUBLANES
    grid_spec = pltpu.PrefetchScalarGridSpec(
        num_scalar_prefetch=2,
        grid=(nt,),
        in_specs=[pl.BlockSpec((t, AUG), lambda i, a, c: (i, 0)),
                  pl.BlockSpec((N_EXPERTS, t), lambda i, a, c: (0, i))],
        out_specs=pl.BlockSpec(memory_space=pl.ANY),
        scratch_shapes=[pltpu.VMEM((2, N_EXPERTS * per, AUG), F32), pltpu.VMEM((N_EXPERTS * SLAB, AUG), F32),
                        pltpu.VMEM((N_EXPERTS * SUBLANES, AUG), F32),
                        pltpu.SemaphoreType.DMA((2,)), pltpu.SemaphoreType.DMA(())],
    )
    return pl.pallas_call(
        functools.partial(_dispatch_kernel, stride, pad, nt),
        name="dispatch",
        grid_spec=grid_spec,
        out_shape=jax.ShapeDtypeStruct((N_EXPERTS * stride, AUG), F32),
        compiler_params=_cparams(("arbitrary",), has_side_effects=True),
    )(starts, counts, aug, mask)


def _ffn_kernel(xe_ref, wg_ref, wu_ref, wd_ref, ye_ref):
    e = pl.program_id(0)
    xa = xe_ref[...]
    xb = xa[:, :D_MODEL].astype(BF16)
    lane = lax.broadcasted_iota(I32, (xa.shape[0], LANES), 1)
    mine = jnp.logical_and((lane & (N_EXPERTS - 1)) == e, lane < AFF_PARTS * N_EXPERTS)
    gate = jnp.sum(jnp.where(mine, xa[:, D_MODEL:], 0.0), axis=1, keepdims=True)
    hid = (_silu(jnp.dot(xb, wg_ref[0], preferred_element_type=F32))
           * jnp.dot(xb, wu_ref[0], preferred_element_type=F32))
    ye_ref[...] = jnp.dot(hid.astype(BF16), wd_ref[0], preferred_element_type=F32) * gate


def _ffn(xe, wg, wu, wd, cap, rows, stride):
    spe = stride // rows
    npe = cap // rows
    wspec = pl.BlockSpec((1, D_MODEL, D_MODEL), lambda e, i: (e, 0, 0))
    return pl.pallas_call(
        _ffn_kernel,
        name="expert_ffn",
        grid=(N_EXPERTS, npe),
        in_specs=[pl.BlockSpec((rows, AUG), lambda e, i: (e * spe + i, 0)), wspec, wspec, wspec],
        out_specs=pl.BlockSpec((rows, D_MODEL), lambda e, i: (e * npe + i, 0)),
        out_shape=jax.ShapeDtypeStruct((N_EXPERTS * cap, D_MODEL), F32),
        compiler_params=_cparams(("parallel", "arbitrary")),
    )(xe, wg, wu, wd)


def _combine_kernel(cap, stride, n_tiles, a_ref, c_ref, x1_ref, mask_ref, ye_hbm, out_ref, yc, yo, sem, sem_o):
    win = SLAB + SUBLANES
    i = pl.program_id(0)
    slot = i % 2

    def window_start(tile, e, k):
        first = a_ref[tile * N_EXPERTS + e] + SLAB * k
        return pl.multiple_of(jnp.minimum(first - (first & (SUBLANES - 1)), cap - win), SUBLANES)

    def window_copy(tile, e, k, buf, s):
        return pltpu.make_async_copy(ye_hbm.at[pl.ds(e * stride + window_start(tile, e, k), win)],
                                     buf.at[pl.ds(e * win, win)], s)

    def wait_all(buf, s):
        pltpu.make_async_copy(ye_hbm.at[pl.ds(0, N_EXPERTS * win)], buf, s).wait()

    @pl.when(i == 0)
    def _():
        for e in range(N_EXPERTS):
            window_copy(i, e, 0, yc.at[slot], sem.at[slot]).start()

    @pl.when(i + 1 < n_tiles)
    def _():
        for e in range(N_EXPERTS):
            window_copy(i + 1, e, 0, yc.at[1 - slot], sem.at[1 - slot]).start()

    starts = [a_ref[i * N_EXPERTS + e] for e in range(N_EXPERTS)]
    counts = [c_ref[i * N_EXPERTS + e] for e in range(N_EXPERTS)]
    rankm = _tile_ranks(mask_ref[...])

    def scattered(k, y):
        p = jnp.concatenate(
            [_rank_rows(rankm[e:e + 1, :], win, window_start(i, e, k) - starts[e], SLAB * k, SLAB * (k + 1))
             .astype(F32) for e in range(N_EXPERTS)], axis=0).astype(BF16)
        y_hi = y.astype(BF16)
        y_lo = (y - y_hi.astype(F32)).astype(BF16)
        tn = (((0,), (0,)), ((), ()))
        return (lax.dot_general(p, y_hi, tn, preferred_element_type=F32)
                + lax.dot_general(p, y_lo, tn, preferred_element_type=F32))

    wait_all(yc.at[slot], sem.at[slot])
    out_ref[...] = x1_ref[...] + scattered(0, yc[slot])

    def extra_window(k, carry):
        for e in range(N_EXPERTS):
            window_copy(i, e, k, yo, sem_o).start()
        wait_all(yo, sem_o)
        out_ref[...] += scattered(k, yo[...])
        return carry

    lax.fori_loop(1, lax.div(_smax(counts) + (SLAB - 1), jnp.int32(SLAB)), extra_window, 0)


def _combine(starts, counts, x1, mask, ye, cap, stride):
    n = x1.shape[0]
    t = DISPATCH_TILE
    nt = n // t
    win = SLAB + SUBLANES
    row = pl.BlockSpec((t, D_MODEL), lambda i, a, c: (i, 0))
    grid_spec = pltpu.PrefetchScalarGridSpec(
        num_scalar_prefetch=2,
        grid=(nt,),
        in_specs=[row, pl.BlockSpec((N_EXPERTS, t), lambda i, a, c: (0, i)), pl.BlockSpec(memory_space=pl.ANY)],
        out_specs=row,
        scratch_shapes=[pltpu.VMEM((2, N_EXPERTS * win, D_MODEL), F32), pltpu.VMEM((N_EXPERTS * win, D_MODEL), F32),
                        pltpu.SemaphoreType.DMA((2,)), pltpu.SemaphoreType.DMA(())],
    )
    return pl.pallas_call(
        functools.partial(_combine_kernel, cap, stride, nt),
        name="combine",
        grid_spec=grid_spec,
        out_shape=jax.ShapeDtypeStruct(x1.shape, F32),
        compiler_params=_cparams(("arbitrary",)),
    )(starts, counts, x1, mask, ye)


def _moe(x1, aug, aff_t, wg, wu, wd, cap):
    rows = max(r for r in range(SLAB + SUBLANES, 513, SUBLANES) if cap % r == 0)
    stride = cap + rows
    mask, starts, counts = _topk(aff_t, cap)
    xe = _dispatch(starts, counts, aug, mask, stride, rows)
    ye = _ffn(xe, wg, wu, wd, cap, rows, stride)
    return _combine(starts, counts, x1, mask, ye, cap, cap)


def _final_kernel(x_ref, g_ref, y_ref):
    x = x_ref[...]
    y_ref[...] = x * lax.rsqrt(jnp.mean(x * x, axis=-1, keepdims=True) + EPS) * g_ref[...]


def _final_norm(x, g):
    n = x.shape[0]
    tm = ROW_TILE
    spec = pl.BlockSpec((tm, D_MODEL), lambda i: (i, 0))
    return pl.pallas_call(
        _final_kernel, name="final_norm", grid=(n // tm,), in_specs=[spec, _const_spec(g.shape)], out_specs=spec,
        out_shape=jax.ShapeDtypeStruct(x.shape, F32), compiler_params=_cparams(("parallel",)),
    )(x, g)


def _pad_heads(w, axis):
    shape = w.shape
    w = w.reshape(shape[:axis] + (N_HEADS, HEAD_DIM) + shape[axis + 1:])
    pad = [(0, 0)] * w.ndim
    pad[axis + 1] = (0, HEAD_PAD - HEAD_DIM)
    return jnp.pad(w, pad).reshape(shape[:axis] + (D_MP,) + shape[axis + 1:])


def _prepare(norm1_g, w_in, conv_w, conv_b, gate_b, pool_w, pool_scale, mh_norm_g, w_out, norm2_g,
             w_router, w_gate, w_up, w_down, final_g):
    depth = w_in.shape[0]
    o1 = D_POOL
    offs = [o1 + j * D_MLSTM for j in range(5)]
    layers = []
    for l in range(depth):
        w = w_in[l]
        proj = [_pad_heads(w[:, offs[j]:offs[j + 1]], 1).astype(BF16) for j in range(4)]
        eye = jnp.eye(len(POOL_WINDOWS), dtype=F32)
        pw = (eye[:, None, :, None] * pool_w[l][:, :, None, :]).reshape(D_POOL, D_POOL)
        wr = jnp.pad(w_router[l], ((0, 0), (0, LANES - N_EXPERTS)))
        wr_hi = wr.astype(BF16)
        layers.append(dict(
            g1=norm1_g[l][None], wu=w[:, :o1].astype(BF16), wqk=jnp.concatenate(proj[:2], axis=1),
            wv=proj[2], wo=proj[3], wgate=w[:, offs[4]:].astype(BF16),
            cw=jnp.concatenate([_pad_heads(conv_w[l][:, :D_MLSTM], 1), _pad_heads(conv_w[l][:, D_MLSTM:], 1)], axis=1),
            cb=jnp.concatenate([_pad_heads(conv_b[l][:D_MLSTM], 0), _pad_heads(conv_b[l][D_MLSTM:], 0)])[None],
            gb=gate_b[l].reshape(1, N_GATE),
            pw=pw.astype(BF16), ps=pool_scale[l][None], mg=_pad_heads(mh_norm_g[l], 0)[None],
            wop=w_out[l][:D_POOL].astype(BF16), wom=_pad_heads(w_out[l][D_POOL:], 0).astype(BF16),
            g2=norm2_g[l][None], wrh=wr_hi, wrl=(wr - wr_hi.astype(F32)).astype(BF16),
            eg=w_gate[l].astype(BF16), eu=w_up[l].astype(BF16), ed=w_down[l].astype(BF16),
        ))
    return layers, final_g[None]


def _trunk(x_in, meta_tokens, layers, final_g):
    n_batch, s, _ = x_in.shape
    seq_len = s + N_META
    lp = -(-(seq_len + HALO) // CHUNK) * CHUNK
    n_tok = n_batch * seq_len
    cap = CAPACITY_FACTOR * n_tok // N_EXPERTS
    meta = jnp.broadcast_to(meta_tokens[None].astype(x_in.dtype), (n_batch, N_META, D_MODEL))
    x = jnp.concatenate([meta, x_in, jnp.zeros((n_batch, lp - seq_len, D_MODEL), x_in.dtype)], axis=1)
    x = x.reshape(n_batch * lp, D_MODEL)
    for p in layers:
        u, q, k, v, o, gates = _inproj(x, p["g1"], p["wu"], p["wqk"], p["wv"], p["wo"], p["wgate"],
                                       p["cw"], p["cb"], lp, seq_len)
        hf, hb = _mlstm(q, k, v, gates, p["gb"], n_batch, lp, seq_len)
        x1, aug, aff_t = _mixout(u, hf, hb, o, x, p["pw"], p["ps"], p["mg"], p["wop"], p["wom"],
                                 p["g2"], p["wrh"], p["wrl"], lp, seq_len)
        x = _moe(x1, aug, aff_t, p["eg"], p["eu"], p["ed"], cap)
    y = _final_norm(x, final_g)
    return y.reshape(n_batch, lp, D_MODEL)[:, N_META:seq_len]


def kernel(x_prompt, x_sample, meta_tokens, norm1_g, w_in, conv_w, conv_b, gate_b, pool_w, pool_scale,
           mh_norm_g, w_out, norm2_g, w_router, w_gate, w_up, w_down, final_g):
    layers, fg = _prepare(norm1_g, w_in, conv_w, conv_b, gate_b, pool_w, pool_scale, mh_norm_g, w_out,
                          norm2_g, w_router, w_gate, w_up, w_down, final_g)
    return (_trunk(x_prompt, meta_tokens, layers, fg), _trunk(x_sample, meta_tokens, layers, fg))
```

```python
import functools

import jax
import jax.numpy as jnp
from jax import lax
from jax.experimental import pallas as pl
from jax.experimental.pallas import tpu as pltpu

F32 = jnp.float32
BF16 = jnp.bfloat16
I32 = jnp.int32

D_MODEL = 1024
N_META = 16
D_POOL = 256
POOL_WINDOWS = (2, 4, 8, 16)
D_POOL_GROUP = 64
N_HEADS = 4
HEAD_DIM = 192
HEAD_PAD = 256
D_MLSTM = N_HEADS * HEAD_DIM
D_MP = N_HEADS * HEAD_PAD
N_GATE = 4 * N_HEADS
N_EXPERTS = 16
CAPACITY_FACTOR = 2
EPS = 1e-6
NEG = -1e30

LANES = 128
SUBLANES = 8
CHUNK = 128
HALO = SUBLANES
XHALO = 2 * SUBLANES
AUG = D_MODEL + LANES
ROW_TILE = 256
QK_BLOCK = 512
AFF_PARTS = 3
DISPATCH_TILE = 256
SLAB = 64
ALIGN = 2 * SUBLANES
MAX_FFN_ROWS = 704
FFN_STEP_ROWS = 48
VMEM_LIMIT = 48 * 1024 * 1024


def _cparams(sem, vmem=VMEM_LIMIT, **kw):
    return pltpu.CompilerParams(dimension_semantics=sem, vmem_limit_bytes=vmem, **kw)


def _const_spec(shape):
    nd = len(shape)
    return pl.BlockSpec(shape, lambda *_: (0,) * nd)


def _sigmoid(y):
    return 0.5 * jnp.tanh(0.5 * y) + 0.5


def _silu(y):
    return y * _sigmoid(y)


def _inproj_kernel(lp, seq_len, n_tiles, x_ref, xprev_ref, xnext_ref, g_ref, wu_ref, wqk_ref, wv_ref, wo_ref,
                   wg_ref, cw_ref, cb_ref, u_ref, q_ref, k_ref, v_ref, o_ref, gt_ref, zs_ref):
    i = pl.program_id(0)
    tm = x_ref.shape[0]
    x = jnp.concatenate([jnp.where(i > 0, xprev_ref[...], 0.0), x_ref[...],
                         jnp.where(i < n_tiles - 1, xnext_ref[...], 0.0)], axis=0)
    hb = (x * lax.rsqrt(jnp.mean(x * x, axis=-1, keepdims=True) + EPS) * g_ref[...]).astype(BF16)
    valid = _positions(i, tm, lp, (tm, 1), 0) < seq_len
    n_blocks = 2 * D_MP // QK_BLOCK

    def project(j):
        zs_ref[j % 2] = jnp.dot(hb, wqk_ref[:, j * QK_BLOCK:(j + 1) * QK_BLOCK], preferred_element_type=F32)

    def activate(j):
        sl = slice(j * QK_BLOCK, (j + 1) * QK_BLOCK)
        z = zs_ref[j % 2]
        n_ext = z.shape[0]
        y = (cw_ref[0:1, sl] * pltpu.roll(z, 1, 0) + cw_ref[1:2, sl] * z
             + cw_ref[2:3, sl] * pltpu.roll(z, n_ext - 1, 0) + cb_ref[:, sl])[XHALO:XHALO + tm]
        y = jnp.where(valid, _silu(y), 0.0)
        if j * QK_BLOCK < D_MP:
            q_ref[:, sl] = (y * (HEAD_DIM ** -0.5)).astype(BF16)
        else:
            k_ref[:, j * QK_BLOCK - D_MP:(j + 1) * QK_BLOCK - D_MP] = y.astype(BF16)

    project(0)
    for j in range(n_blocks):
        if j + 1 < n_blocks:
            project(j + 1)
        else:
            hc = hb[XHALO:XHALO + tm]
            for w_ref, out_ref in ((wu_ref, u_ref), (wv_ref, v_ref), (wo_ref, o_ref), (wg_ref, gt_ref)):
                out_ref[...] = jnp.dot(hc, w_ref[...], preferred_element_type=F32).astype(out_ref.dtype)
        activate(j)


def _inproj(x, g, wu, wqk, wv, wo, wg, cw, cb, lp, seq_len):
    n = x.shape[0]
    tm = ROW_TILE
    nt = n // tm
    hpt = tm // XHALO
    row = lambda w: pl.BlockSpec((tm, w), lambda i: (i, 0))
    prev = pl.BlockSpec((XHALO, D_MODEL), lambda i: (jnp.maximum(i * hpt - 1, 0), 0))
    nxt = pl.BlockSpec((XHALO, D_MODEL), lambda i: (jnp.minimum((i + 1) * hpt, n // XHALO - 1), 0))
    consts = (g, wu, wqk, wv, wo, wg, cw, cb)
    outs = ((D_POOL, F32), (D_MP, BF16), (D_MP, BF16), (D_MP, BF16), (D_MP, F32), (N_GATE, F32))
    return pl.pallas_call(
        functools.partial(_inproj_kernel, lp, seq_len, nt),
        name="inproj",
        grid=(nt,),
        in_specs=[row(D_MODEL), prev, nxt] + [_const_spec(w.shape) for w in consts],
        out_specs=[row(w) for w, _ in outs],
        out_shape=[jax.ShapeDtypeStruct((n, w), dt) for w, dt in outs],
        scratch_shapes=[pltpu.VMEM((2, tm + 2 * XHALO, QK_BLOCK), F32)],
        compiler_params=_cparams(("parallel",)),
    )(x, x, x, *consts)


def _log_sigmoid(x):
    return jnp.minimum(x, 0.0) - jnp.log1p(jnp.exp(-jnp.abs(x)))


def _mlstm_chunk(rev, valid, q_ref, k_ref, v_ref, gates, h_ref, c_ref, n_ref, m_ref):
    t_i = lax.broadcasted_iota(I32, (CHUNK, CHUNK), 0)
    s_i = lax.broadcasted_iota(I32, (CHUNK, CHUNK), 1)
    eye = t_i == s_i
    tri = (s_i >= t_i) if rev else (s_i <= t_i)
    tri_t = (t_i >= s_i) if rev else (t_i <= s_i)

    def to_row(col):
        return jnp.sum(jnp.where(eye, col, 0.0), axis=0, keepdims=True)

    log_i = jnp.where(valid, gates, NEG)
    log_f = jnp.where(valid, _log_sigmoid(gates), 0.0)
    for h in range(N_HEADS):
        sl = slice(h * HEAD_PAD, (h + 1) * HEAD_PAD)
        st = (N_HEADS if rev else 0) + h
        ji = (N_HEADS if rev else 0) + h
        jf = (3 * N_HEADS if rev else 2 * N_HEADS) + h
        li_c = log_i[:, ji:ji + 1]
        lf_c = log_f[:, jf:jf + 1]
        li_r = to_row(li_c)
        lf_r = to_row(lf_c)
        b_c = jnp.sum(jnp.where(tri, lf_r, 0.0), axis=1, keepdims=True)
        b_r = jnp.sum(jnp.where(tri_t, lf_c, 0.0), axis=0, keepdims=True)
        b_tot = jnp.sum(lf_c, axis=0, keepdims=True)
        m_prev = m_ref[st]
        n_prev = n_ref[st]

        a_c = b_tot - b_c + li_c
        m_loc = jnp.max(a_c, axis=0, keepdims=True)
        wa_c = jnp.exp(a_c - m_loc)

        dlog = jnp.where(tri, b_c - b_r + li_r, -jnp.inf)
        inter = b_c + m_prev
        m_t = jnp.maximum(inter, jnp.max(dlog, axis=1, keepdims=True))
        w_inter = jnp.exp(inter - m_t)
        dexp = jnp.exp(dlog - m_t)

        qb = q_ref[:, sl]
        kb = k_ref[:, sl]
        vb = v_ref[:, sl]
        scores = lax.dot_general(qb, kb, (((1,), (1,)), ((), ())), preferred_element_type=F32) * dexp
        num = (w_inter * jnp.dot(qb, c_ref[st].astype(BF16), preferred_element_type=F32)
               + jnp.dot(scores.astype(BF16), vb, preferred_element_type=F32))
        den = (w_inter * jnp.sum(qb.astype(F32) * n_prev, axis=1, keepdims=True)
               + jnp.sum(scores, axis=1, keepdims=True))
        hh = num / jnp.maximum(jnp.abs(den), jnp.exp(-m_t))
        h_ref[:, sl] = jnp.where(valid, hh, 0.0)

        m_new = jnp.maximum(b_tot + m_prev, m_loc)
        s_prev = jnp.exp(b_tot + m_prev - m_new)
        s_loc = jnp.exp(m_loc - m_new)
        kw = kb.astype(F32) * wa_c
        c_loc = lax.dot_general(kw.astype(BF16), vb, (((0,), (0,)), ((), ())), preferred_element_type=F32)
        c_ref[st] = s_prev * c_ref[st] + s_loc * c_loc
        n_ref[st] = s_prev * n_prev + s_loc * jnp.sum(kw, axis=0, keepdims=True)
        m_ref[st] = m_new


def _mlstm_kernel(seq_len, n_chunks, qf_ref, kf_ref, vf_ref, gf_ref, qb_ref, kb_ref, vb_ref, gb_ref, bias_ref,
                  hf_ref, hb_ref, c_ref, n_ref, m_ref):
    c = pl.program_id(1)

    @pl.when(c == 0)
    def _():
        c_ref[...] = jnp.zeros_like(c_ref)
        n_ref[...] = jnp.zeros_like(n_ref)
        m_ref[...] = jnp.zeros_like(m_ref)

    row = lax.broadcasted_iota(I32, (CHUNK, 1), 0)
    _mlstm_chunk(False, (c * CHUNK + row) < seq_len, qf_ref, kf_ref, vf_ref, gf_ref[...] + bias_ref[...],
                 hf_ref, c_ref, n_ref, m_ref)
    _mlstm_chunk(True, ((n_chunks - 1 - c) * CHUNK + row) < seq_len, qb_ref, kb_ref, vb_ref,
                 gb_ref[...] + bias_ref[...], hb_ref, c_ref, n_ref, m_ref)


def _mlstm(q, k, v, gates, bias, n_batch, lp, seq_len):
    n = q.shape[0]
    nc = lp // CHUNK
    fwd = lambda w: pl.BlockSpec((CHUNK, w), lambda b, c: (b * nc + c, 0))
    bwd = lambda w: pl.BlockSpec((CHUNK, w), lambda b, c: (b * nc + nc - 1 - c, 0))
    return pl.pallas_call(
        functools.partial(_mlstm_kernel, seq_len, nc),
        name="mlstm",
        grid=(n_batch, nc),
        in_specs=[fwd(D_MP), fwd(D_MP), fwd(D_MP), fwd(N_GATE), bwd(D_MP), bwd(D_MP), bwd(D_MP), bwd(N_GATE),
                  _const_spec(bias.shape)],
        out_specs=[fwd(D_MP), bwd(D_MP)],
        out_shape=[jax.ShapeDtypeStruct((n, D_MP), F32), jax.ShapeDtypeStruct((n, D_MP), F32)],
        scratch_shapes=[pltpu.VMEM((2 * N_HEADS, HEAD_PAD, HEAD_PAD), F32),
                        pltpu.VMEM((2 * N_HEADS, 1, HEAD_PAD), F32),
                        pltpu.VMEM((2 * N_HEADS, 1, 1), F32)],
        compiler_params=_cparams(("parallel", "arbitrary")),
    )(q, k, v, gates, q, k, v, gates, bias)


def _positions(i, tm, lp, shape, axis):
    row0 = i * tm
    p0 = row0 - (row0 // lp) * lp
    p = p0 + lax.broadcasted_iota(I32, shape, axis)
    return jnp.where(p >= lp, p - lp, p)


def _mixout_kernel(lp, seq_len, n_tiles,
                   u_ref, uprev_ref, unext_ref, hf_ref, hb_ref, o_ref, x_ref,
                   pw_ref, ps_ref, mg_ref, wop_ref, wom_ref, g2_ref, wrh_ref, wrl_ref,
                   x1_ref, aug_ref, afft_ref):
    i = pl.program_id(0)
    tm = x_ref.shape[0]
    pos = _positions(i, tm, lp, (tm, 1), 0)
    valid = pos < seq_len

    u = u_ref[...]
    ext = jnp.concatenate([jnp.where(i > 0, uprev_ref[...], 0.0), u,
                           jnp.where(i < n_tiles - 1, unext_ref[...], 0.0)], axis=0)
    n_ext = tm + 2 * HALO

    def shifted(a, d):
        return pltpu.roll(a, (-d) % n_ext, 0)

    w2 = shifted(ext, -1) + ext
    w4 = shifted(w2, -1) + shifted(w2, 1)
    w8 = shifted(w4, -2) + shifted(w4, 2)
    w16 = shifted(w8, -4) + shifted(w8, 4)
    lane = lax.broadcasted_iota(I32, (tm, D_POOL), 1)
    grp = lane // D_POOL_GROUP
    centre = slice(HALO, HALO + tm)
    wsum = jnp.where(grp == 0, w2[centre], jnp.where(grp == 1, w4[centre],
                     jnp.where(grp == 2, w8[centre], w16[centre])))
    half = jnp.where(grp == 0, 1, jnp.where(grp == 1, 2, jnp.where(grp == 2, 4, 8)))
    lo = jnp.clip(pos - half, 0, seq_len)
    hi = jnp.clip(pos + half, 0, seq_len)
    cnt = jnp.maximum(hi - lo, 1).astype(F32)
    mixed = jnp.where(valid, wsum / cnt - u, 0.0)
    y_pool = jnp.dot(mixed.astype(BF16), pw_ref[...], preferred_element_type=F32) * ps_ref[...]

    hs = hf_ref[...] + hb_ref[...]
    og = _sigmoid(o_ref[...])
    parts = []
    for h in range(N_HEADS):
        sl = slice(h * HEAD_PAD, (h + 1) * HEAD_PAD)
        hh = hs[:, sl]
        ms = jnp.sum(hh * hh, axis=-1, keepdims=True) * (1.0 / HEAD_DIM)
        parts.append(hh * lax.rsqrt(ms + EPS))
    y_m = jnp.concatenate(parts, axis=1) * mg_ref[...] * og
    y_m = jnp.where(valid, y_m, 0.0)

    x1 = (x_ref[...] + jnp.dot(y_pool.astype(BF16), wop_ref[...], preferred_element_type=F32)
          + jnp.dot(y_m.astype(BF16), wom_ref[...], preferred_element_type=F32))
    x1_ref[...] = x1

    h2 = x1 * lax.rsqrt(jnp.mean(x1 * x1, axis=-1, keepdims=True) + EPS) * g2_ref[...]
    h_hi = h2.astype(BF16)
    h_lo = (h2 - h_hi.astype(F32)).astype(BF16)
    logits = (jnp.dot(h_hi, wrh_ref[...], preferred_element_type=F32)
              + jnp.dot(h_lo, wrh_ref[...], preferred_element_type=F32)
              + jnp.dot(h_hi, wrl_ref[...], preferred_element_type=F32))
    lane = lax.broadcasted_iota(I32, (tm, LANES), 1)
    logits = jnp.where(lane < N_EXPERTS, logits, -jnp.inf)
    e = jnp.exp(logits - jnp.max(logits, axis=-1, keepdims=True))
    aff = e / jnp.sum(e, axis=-1, keepdims=True)
    hi = aff.astype(BF16).astype(F32)
    mid = (aff - hi).astype(BF16).astype(F32)
    lo = aff - hi - mid
    parts = (hi + pltpu.roll(mid, N_EXPERTS, 1) + pltpu.roll(lo, 2 * N_EXPERTS, 1)).astype(BF16)
    aug_ref[:, :D_MODEL] = h_hi
    aug_ref[:, D_MODEL:] = parts

    e_i = lax.broadcasted_iota(I32, (N_EXPERTS, LANES), 0)
    l_i = lax.broadcasted_iota(I32, (N_EXPERTS, LANES), 1)
    pick = jnp.logical_and((l_i & (N_EXPERTS - 1)) == e_i, l_i < AFF_PARTS * N_EXPERTS).astype(BF16)
    aff_t = lax.dot_general(pick, parts, (((1,), (1,)), ((), ())), preferred_element_type=F32)
    valid_r = _positions(i, tm, lp, (1, tm), 1) < seq_len
    afft_ref[...] = jnp.where(valid_r, aff_t, -1.0)


def _mixout(u, hf, hb, o, x, pw, ps, mg, wop, wom, g2, wrh, wrl, lp, seq_len):
    n = x.shape[0]
    tm = ROW_TILE
    nt = n // tm
    hpt = tm // HALO
    row = lambda w: pl.BlockSpec((tm, w), lambda i: (i, 0))
    prev = pl.BlockSpec((HALO, D_POOL), lambda i: (jnp.maximum(i * hpt - 1, 0), 0))
    nxt = pl.BlockSpec((HALO, D_POOL), lambda i: (jnp.minimum((i + 1) * hpt, n // HALO - 1), 0))
    consts = (pw, ps, mg, wop, wom, g2, wrh, wrl)
    return pl.pallas_call(
        functools.partial(_mixout_kernel, lp, seq_len, nt),
        name="mixout",
        grid=(nt,),
        in_specs=[row(D_POOL), prev, nxt, row(D_MP), row(D_MP), row(D_MP), row(D_MODEL)]
                 + [_const_spec(w.shape) for w in consts],
        out_specs=[row(D_MODEL), row(AUG), pl.BlockSpec((N_EXPERTS, tm), lambda i: (0, i))],
        out_shape=[jax.ShapeDtypeStruct((n, D_MODEL), F32), jax.ShapeDtypeStruct((n, AUG), BF16),
                   jax.ShapeDtypeStruct((N_EXPERTS, n), F32)],
        compiler_params=_cparams(("parallel",)),
    )(u, u, u, hf, hb, o, x, *consts)


def _topk_kernel(cap, aff_ref, mask_ref, bpx_ref):
    ne, nb, _ = aff_ref.shape
    bits = lax.bitcast_convert_type(aff_ref[...], I32)

    def count(pred):
        return jnp.sum(jnp.sum(pred.astype(I32), axis=2, keepdims=True), axis=1, keepdims=True)

    def bisect(it, thr):
        cand = thr | (jnp.int32(1) << (30 - it))
        return jnp.where(count(bits >= cand) >= cap, cand, thr)

    thr = lax.fori_loop(0, 31, bisect, jnp.zeros((ne, 1, 1), I32))
    gt = bits > thr
    eq = bits == thr
    need = cap - count(gt)

    i_a = lax.broadcasted_iota(I32, (LANES, LANES), 0)
    i_b = lax.broadcasted_iota(I32, (LANES, LANES), 1)
    upper = (i_a <= i_b).astype(BF16)
    k_a = lax.broadcasted_iota(I32, (nb, nb), 0)
    k_b = lax.broadcasted_iota(I32, (nb, nb), 1)
    strict_lower = (k_b < k_a).astype(BF16)

    def block_prefix(local_incl):
        totals = jnp.broadcast_to(local_incl[:, LANES - 1:LANES], (nb, LANES)).astype(BF16)
        return jnp.dot(strict_lower, totals, preferred_element_type=F32)

    for e in range(ne):
        eq_e = eq[e].astype(BF16)
        eq_rank = jnp.dot(eq_e, upper, preferred_element_type=F32)
        eq_rank = eq_rank + block_prefix(eq_rank)
        need_e = need[e].astype(F32)
        sel = jnp.logical_or(gt[e], jnp.logical_and(eq[e], eq_rank <= need_e)).astype(BF16)
        mask_ref[e] = sel
        bpx_ref[e] = block_prefix(jnp.dot(sel, upper, preferred_element_type=F32)).astype(I32)


def _topk(aff_t, cap):
    ne, n = aff_t.shape
    nb = n // LANES
    nbp = -(-nb // LANES) * LANES
    aff3 = jnp.pad(aff_t.reshape(ne, nb, LANES), ((0, 0), (0, nbp - nb), (0, 0)), constant_values=-1.0)
    mask, bpx = pl.pallas_call(
        functools.partial(_topk_kernel, cap),
        name="topk",
        grid=(1,),
        in_specs=[_const_spec(aff3.shape)],
        out_specs=[_const_spec(aff3.shape), _const_spec(aff3.shape)],
        out_shape=[jax.ShapeDtypeStruct(aff3.shape, BF16), jax.ShapeDtypeStruct(aff3.shape, I32)],
        compiler_params=_cparams(("arbitrary",)),
    )(aff3)
    starts = bpx[:, 0:nb:DISPATCH_TILE // LANES, 0]
    counts = jnp.concatenate([starts[:, 1:], jnp.full((ne, 1), cap, I32)], axis=1) - starts
    return mask.reshape(ne, nbp * LANES), starts.T.reshape(-1), counts.T.reshape(-1)


def _tile_ranks(m):
    t = m.shape[1]
    before = (lax.broadcasted_iota(I32, (t, t), 0) < lax.broadcasted_iota(I32, (t, t), 1)).astype(BF16)
    rank = jnp.dot(m, before, preferred_element_type=F32)
    return jnp.where(m > 0, rank, -1.0)


def _f(x):
    return jnp.asarray(x, I32).astype(F32)


def _rank_rows(r, n_rows, off, lo=0, hi=None):
    w = lax.broadcasted_iota(I32, (n_rows, r.shape[1]), 0).astype(F32)
    hit = jnp.logical_and(r == w + _f(off), r >= _f(lo))
    if hi is not None:
        hit = jnp.logical_and(hit, r < _f(hi))
    return hit


def _smax(xs):
    most = xs[0]
    for x in xs[1:]:
        most = jnp.maximum(most, x)
    return most


def _dispatch_kernel(stride, pad, n_tiles, a_ref, c_ref, aug_ref, mask_ref, xe_hbm, xc, xb, xo, carry, sem, sem_o):
    win = SLAB + ALIGN
    per = win + ALIGN
    i = pl.program_id(0)
    slot = i % 2
    start = [a_ref[i * N_EXPERTS + e] for e in range(N_EXPERTS)]
    count = [c_ref[i * N_EXPERTS + e] for e in range(N_EXPERTS)]
    head = [a & (ALIGN - 1) for a in start]
    base = [a - h for a, h in zip(start, head)]
    end = [a + c for a, c in zip(start, count)]
    head2 = [x & (ALIGN - 1) for x in end]
    base2 = [x - h for x, h in zip(end, head2)]
    rankm = _tile_ranks(mask_ref[...])
    aug = aug_ref[...]

    @pl.when(i == 0)
    def _():
        carry[...] = jnp.zeros_like(carry)
        xb[0] = jnp.zeros(xb.shape[1:], xb.dtype)
        pads = [pltpu.make_async_copy(xb.at[0, pl.ds(0, pad)], xe_hbm.at[pl.ds(e * stride + stride - pad, pad)], sem_o)
                for e in range(N_EXPERTS)]
        for cp in pads:
            cp.start()
        for cp in pads:
            cp.wait()

    parts = []
    for e in range(N_EXPERTS):
        r = rankm[e:e + 1, :]
        sub = lax.broadcasted_iota(I32, (ALIGN, r.shape[1]), 0)
        nxt = jnp.logical_and(_rank_rows(r, ALIGN, base2[e] - start[e]), sub < head2[e])
        parts += [_rank_rows(r, win, -head[e]).astype(F32), nxt.astype(F32)]
    xc[...] = jnp.dot(jnp.concatenate(parts, axis=0).astype(BF16), aug, preferred_element_type=F32)
    for e in range(N_EXPERTS):
        old = carry[e * ALIGN:(e + 1) * ALIGN, :]
        carry[e * ALIGN:(e + 1) * ALIGN, :] = (xc[e * per + win:(e + 1) * per, :]
                                               + jnp.where(base2[e] == base[e], old, 0.0))
        xb[slot, pl.ds(e * win, ALIGN), :] = (xc[e * per:e * per + ALIGN, :] + old).astype(xb.dtype)
        xb[slot, pl.ds(e * win + ALIGN, SLAB), :] = xc[e * per + ALIGN:e * per + win, :].astype(xb.dtype)

    def wait_slot(sl):
        pltpu.make_async_copy(xb.at[sl], xe_hbm.at[pl.ds(0, N_EXPERTS * win)], sem.at[sl]).wait()

    @pl.when(i > 0)
    def _():
        wait_slot(1 - slot)

    for e in range(N_EXPERTS):
        dst = e * stride + pl.multiple_of(base[e], ALIGN)
        pltpu.make_async_copy(xb.at[slot, pl.ds(e * win, win)], xe_hbm.at[pl.ds(dst, win)], sem.at[slot]).start()

    def extra_copy(e, k):
        dst = e * stride + pl.multiple_of(base[e] + win + SLAB * (k - 1), ALIGN)
        return pltpu.make_async_copy(xo.at[pl.ds(e * SLAB, SLAB)], xe_hbm.at[pl.ds(dst, SLAB)], sem_o)

    def extra_window(k, _):
        p = [_rank_rows(rankm[e:e + 1, :], SLAB, win - head[e] + SLAB * (k - 1)).astype(F32)
             for e in range(N_EXPERTS)]
        xo[...] = jnp.dot(jnp.concatenate(p, axis=0).astype(BF16), aug, preferred_element_type=F32).astype(xo.dtype)
        for e in range(N_EXPERTS):
            @pl.when(count[e] + head[e] > win + SLAB * (k - 1))
            def _():
                extra_copy(e, k).start()
        for e in range(N_EXPERTS):
            @pl.when(count[e] + head[e] > win + SLAB * (k - 1))
            def _():
                extra_copy(e, k).wait()
        return 0

    most = _smax([c + h for c, h in zip(count, head)])
    n_extra = lax.div(jnp.maximum(most - win, 0) + (SLAB - 1), jnp.int32(SLAB))
    lax.fori_loop(1, n_extra + 1, extra_window, 0)

    @pl.when(i == n_tiles - 1)
    def _():
        wait_slot(slot)


def _dispatch(starts, counts, aug, mask, stride, pad):
    win = SLAB + ALIGN
    assert SLAB % ALIGN == 0 and pad <= N_EXPERTS * win
    n = aug.shape[0]
    t = DISPATCH_TILE
    nt = n // t
    grid_spec = pltpu.PrefetchScalarGridSpec(
        num_scalar_prefetch=2,
        grid=(nt,),
        in_specs=[pl.BlockSpec((t, AUG), lambda i, a, c: (i, 0)),
                  pl.BlockSpec((N_EXPERTS, t), lambda i, a, c: (0, i))],
        out_specs=pl.BlockSpec(memory_space=pl.ANY),
        scratch_shapes=[pltpu.VMEM((N_EXPERTS * (win + ALIGN), AUG), F32), pltpu.VMEM((2, N_EXPERTS * win, AUG), BF16),
                        pltpu.VMEM((N_EXPERTS * SLAB, AUG), BF16), pltpu.VMEM((N_EXPERTS * ALIGN, AUG), F32),
                        pltpu.SemaphoreType.DMA((2,)), pltpu.SemaphoreType.DMA(())],
    )
    return pl.pallas_call(
        functools.partial(_dispatch_kernel, stride, pad, nt),
        name="dispatch",
        grid_spec=grid_spec,
        out_shape=jax.ShapeDtypeStruct((N_EXPERTS * stride, AUG), BF16),
        compiler_params=_cparams(("arbitrary",), has_side_effects=True),
    )(starts, counts, aug, mask)


def _ffn_kernel(xe_ref, wg_ref, wu_ref, wd_ref, ye_ref):
    e = pl.program_id(0)
    xb = xe_ref[:, :D_MODEL]
    lane = lax.broadcasted_iota(I32, (xb.shape[0], LANES), 1)
    mine = jnp.logical_and((lane & (N_EXPERTS - 1)) == e, lane < AFF_PARTS * N_EXPERTS)
    gate = jnp.sum(jnp.where(mine, xe_ref[:, D_MODEL:].astype(F32), 0.0), axis=1, keepdims=True)
    hid = (_silu(jnp.dot(xb, wg_ref[0], preferred_element_type=F32))
           * jnp.dot(xb, wu_ref[0], preferred_element_type=F32))
    ye_ref[...] = (jnp.dot(hid.astype(BF16), wd_ref[0], preferred_element_type=F32) * gate).astype(ye_ref.dtype)


def _ffn(xe, wg, wu, wd, rows, npe):
    wspec = pl.BlockSpec((1, D_MODEL, D_MODEL), lambda e, i: (e, 0, 0))
    return pl.pallas_call(
        _ffn_kernel,
        name="expert_ffn",
        grid=(N_EXPERTS, npe),
        in_specs=[pl.BlockSpec((rows, AUG), lambda e, i: (e * (npe + 1) + i, 0)), wspec, wspec, wspec],
        out_specs=pl.BlockSpec((rows, D_MODEL), lambda e, i: (e * npe + i, 0)),
        out_shape=jax.ShapeDtypeStruct((N_EXPERTS * npe * rows, D_MODEL), BF16),
        compiler_params=_cparams(("parallel", "arbitrary")),
    )(xe, wg, wu, wd)


def _combine_kernel(stride, n_tiles, a_ref, c_ref, x1_ref, mask_ref, ye_hbm, out_ref, yc, yo, sem, sem_o):
    win = SLAB + ALIGN
    i = pl.program_id(0)
    slot = i % 2

    def window_start(tile, e, k):
        first = a_ref[tile * N_EXPERTS + e] + SLAB * k
        return pl.multiple_of(jnp.minimum(first - (first & (ALIGN - 1)), stride - win), ALIGN)

    def window_copy(tile, e, k, buf, s):
        return pltpu.make_async_copy(ye_hbm.at[pl.ds(e * stride + window_start(tile, e, k), win)],
                                     buf.at[pl.ds(e * win, win)], s)

    def wait_all(buf, s):
        pltpu.make_async_copy(ye_hbm.at[pl.ds(0, N_EXPERTS * win)], buf, s).wait()

    @pl.when(i == 0)
    def _():
        for e in range(N_EXPERTS):
            window_copy(i, e, 0, yc.at[slot], sem.at[slot]).start()

    @pl.when(i + 1 < n_tiles)
    def _():
        for e in range(N_EXPERTS):
            window_copy(i + 1, e, 0, yc.at[1 - slot], sem.at[1 - slot]).start()

    starts = [a_ref[i * N_EXPERTS + e] for e in range(N_EXPERTS)]
    counts = [c_ref[i * N_EXPERTS + e] for e in range(N_EXPERTS)]
    rankm = _tile_ranks(mask_ref[...])

    def scattered(k, y):
        p = jnp.concatenate(
            [_rank_rows(rankm[e:e + 1, :], win, window_start(i, e, k) - starts[e], SLAB * k, SLAB * (k + 1))
             .astype(F32) for e in range(N_EXPERTS)], axis=0).astype(BF16)
        return lax.dot_general(p, y, (((0,), (0,)), ((), ())), preferred_element_type=F32)

    wait_all(yc.at[slot], sem.at[slot])
    out_ref[...] = x1_ref[...] + scattered(0, yc[slot])

    def extra_window(k, carry):
        for e in range(N_EXPERTS):
            window_copy(i, e, k, yo, sem_o).start()
        wait_all(yo, sem_o)
        out_ref[...] += scattered(k, yo[...])
        return carry

    lax.fori_loop(1, lax.div(_smax(counts) + (SLAB - 1), jnp.int32(SLAB)), extra_window, 0)


def _combine(starts, counts, x1, mask, ye, stride):
    n = x1.shape[0]
    t = DISPATCH_TILE
    nt = n // t
    win = SLAB + ALIGN
    row = pl.BlockSpec((t, D_MODEL), lambda i, a, c: (i, 0))
    grid_spec = pltpu.PrefetchScalarGridSpec(
        num_scalar_prefetch=2,
        grid=(nt,),
        in_specs=[row, pl.BlockSpec((N_EXPERTS, t), lambda i, a, c: (0, i)), pl.BlockSpec(memory_space=pl.ANY)],
        out_specs=row,
        scratch_shapes=[pltpu.VMEM((2, N_EXPERTS * win, D_MODEL), BF16), pltpu.VMEM((N_EXPERTS * win, D_MODEL), BF16),
                        pltpu.SemaphoreType.DMA((2,)), pltpu.SemaphoreType.DMA(())],
    )
    return pl.pallas_call(
        functools.partial(_combine_kernel, stride, nt),
        name="combine",
        grid_spec=grid_spec,
        out_shape=jax.ShapeDtypeStruct(x1.shape, F32),
        compiler_params=_cparams(("arbitrary",)),
    )(starts, counts, x1, mask, ye)


def _moe(x1, aug, aff_t, wg, wu, wd, cap):
    win = SLAB + ALIGN
    rows = min(range(win, MAX_FFN_ROWS + 1, ALIGN), key=lambda r: -(-cap // r) * (r + FFN_STEP_ROWS))
    npe = -(-cap // rows)
    stride = (npe + 1) * rows
    mask, starts, counts = _topk(aff_t, cap)
    xe = _dispatch(starts, counts, aug, mask, stride, stride - cap // ALIGN * ALIGN)
    ye = _ffn(xe, wg, wu, wd, rows, npe)
    return _combine(starts, counts, x1, mask, ye, npe * rows)


def _final_kernel(x_ref, g_ref, y_ref):
    x = x_ref[...]
    y_ref[...] = x * lax.rsqrt(jnp.mean(x * x, axis=-1, keepdims=True) + EPS) * g_ref[...]


def _final_norm(x, g):
    n = x.shape[0]
    tm = ROW_TILE
    spec = pl.BlockSpec((tm, D_MODEL), lambda i: (i, 0))
    return pl.pallas_call(
        _final_kernel, name="final_norm", grid=(n // tm,), in_specs=[spec, _const_spec(g.shape)], out_specs=spec,
        out_shape=jax.ShapeDtypeStruct(x.shape, F32), compiler_params=_cparams(("parallel",)),
    )(x, g)


def _pad_heads(w, axis):
    shape = w.shape
    w = w.reshape(shape[:axis] + (N_HEADS, HEAD_DIM) + shape[axis + 1:])
    pad = [(0, 0)] * w.ndim
    pad[axis + 1] = (0, HEAD_PAD - HEAD_DIM)
    return jnp.pad(w, pad).reshape(shape[:axis] + (D_MP,) + shape[axis + 1:])


def _prepare(norm1_g, w_in, conv_w, conv_b, gate_b, pool_w, pool_scale, mh_norm_g, w_out, norm2_g,
             w_router, w_gate, w_up, w_down, final_g):
    depth = w_in.shape[0]
    o1 = D_POOL
    offs = [o1 + j * D_MLSTM for j in range(5)]
    layers = []
    for l in range(depth):
        w = w_in[l]
        proj = [_pad_heads(w[:, offs[j]:offs[j + 1]], 1).astype(BF16) for j in range(4)]
        eye = jnp.eye(len(POOL_WINDOWS), dtype=F32)
        pw = (eye[:, None, :, None] * pool_w[l][:, :, None, :]).reshape(D_POOL, D_POOL)
        wr = jnp.pad(w_router[l], ((0, 0), (0, LANES - N_EXPERTS)))
        wr_hi = wr.astype(BF16)
        layers.append(dict(
            g1=norm1_g[l][None], wu=w[:, :o1].astype(BF16), wqk=jnp.concatenate(proj[:2], axis=1),
            wv=proj[2], wo=proj[3], wgate=w[:, offs[4]:].astype(BF16),
            cw=jnp.concatenate([_pad_heads(conv_w[l][:, :D_MLSTM], 1), _pad_heads(conv_w[l][:, D_MLSTM:], 1)], axis=1),
            cb=jnp.concatenate([_pad_heads(conv_b[l][:D_MLSTM], 0), _pad_heads(conv_b[l][D_MLSTM:], 0)])[None],
            gb=gate_b[l].reshape(1, N_GATE),
            pw=pw.astype(BF16), ps=pool_scale[l][None], mg=_pad_heads(mh_norm_g[l], 0)[None],
            wop=w_out[l][:D_POOL].astype(BF16), wom=_pad_heads(w_out[l][D_POOL:], 0).astype(BF16),
            g2=norm2_g[l][None], wrh=wr_hi, wrl=(wr - wr_hi.astype(F32)).astype(BF16),
            eg=w_gate[l].astype(BF16), eu=w_up[l].astype(BF16), ed=w_down[l].astype(BF16),
        ))
    return layers, final_g[None]


def _trunk(x_in, meta_tokens, layers, final_g):
    n_batch, s, _ = x_in.shape
    seq_len = s + N_META
    lp = -(-(seq_len + HALO) // CHUNK) * CHUNK
    n_tok = n_batch * seq_len
    cap = CAPACITY_FACTOR * n_tok // N_EXPERTS
    meta = jnp.broadcast_to(meta_tokens[None].astype(x_in.dtype), (n_batch, N_META, D_MODEL))
    x = jnp.concatenate([meta, x_in, jnp.zeros((n_batch, lp - seq_len, D_MODEL), x_in.dtype)], axis=1)
    x = x.reshape(n_batch * lp, D_MODEL)
    for p in layers:
        u, q, k, v, o, gates = _inproj(x, p["g1"], p["wu"], p["wqk"], p["wv"], p["wo"], p["wgate"],
                                       p["cw"], p["cb"], lp, seq_len)
        hf, hb = _mlstm(q, k, v, gates, p["gb"], n_batch, lp, seq_len)
        x1, aug, aff_t = _mixout(u, hf, hb, o, x, p["pw"], p["ps"], p["mg"], p["wop"], p["wom"],
                                 p["g2"], p["wrh"], p["wrl"], lp, seq_len)
        x = _moe(x1, aug, aff_t, p["eg"], p["eu"], p["ed"], cap)
    y = _final_norm(x, final_g)
    return y.reshape(n_batch, lp, D_MODEL)[:, N_META:seq_len]


def kernel(x_prompt, x_sample, meta_tokens, norm1_g, w_in, conv_w, conv_b, gate_b, pool_w, pool_scale,
           mh_norm_g, w_out, norm2_g, w_router, w_gate, w_up, w_down, final_g):
    layers, fg = _prepare(norm1_g, w_in, conv_w, conv_b, gate_b, pool_w, pool_scale, mh_norm_g, w_out,
                          norm2_g, w_router, w_gate, w_up, w_down, final_g)
    return (_trunk(x_prompt, meta_tokens, layers, fg), _trunk(x_sample, meta_tokens, layers, fg))
```

```python
import functools

import jax
import jax.numpy as jnp
from jax import lax
from jax.experimental import pallas as pl
from jax.experimental.pallas import tpu as pltpu

F32 = jnp.float32
BF16 = jnp.bfloat16
I32 = jnp.int32

D_MODEL = 1024
N_META = 16
D_POOL = 256
POOL_WINDOWS = (2, 4, 8, 16)
D_POOL_GROUP = 64
N_HEADS = 4
HEAD_DIM = 192
HEAD_PAD = 256
D_MLSTM = N_HEADS * HEAD_DIM
D_MP = N_HEADS * HEAD_PAD
N_GATE = 4 * N_HEADS
N_EXPERTS = 16
CAPACITY_FACTOR = 2
EPS = 1e-6
NEG = -1e30

LANES = 128
SUBLANES = 8
CHUNK = 128
HALO = SUBLANES
XHALO = 2 * SUBLANES
AUG = D_MODEL + LANES
ROW_TILE = 256
QK_BLOCK = 512
MIX_TILE = 256
AFF_PARTS = 3
DISPATCH_TILE = 256
SLAB = 64
ALIGN = 2 * SUBLANES
MAX_FFN_ROWS = 704
FFN_STEP_ROWS = 48
VMEM_LIMIT = 48 * 1024 * 1024
FFN_VMEM_LIMIT = 56 * 1024 * 1024


def _cparams(sem, vmem=VMEM_LIMIT, **kw):
    return pltpu.CompilerParams(dimension_semantics=sem, vmem_limit_bytes=vmem, **kw)


def _const_spec(shape):
    nd = len(shape)
    return pl.BlockSpec(shape, lambda *_: (0,) * nd)


def _sigmoid(y):
    return 0.5 * jnp.tanh(0.5 * y) + 0.5


def _silu(y):
    return y * _sigmoid(y)


def _inproj_kernel(lp, seq_len, n_tiles, x_ref, xprev_ref, xnext_ref, g_ref, wu_ref, wqk_ref, wv_ref, wo_ref,
                   wg_ref, cw_ref, cb_ref, u_ref, q_ref, k_ref, v_ref, o_ref, gt_ref, zs_ref):
    i = pl.program_id(0)
    tm = x_ref.shape[0]
    x = jnp.concatenate([jnp.where(i > 0, xprev_ref[...], 0.0), x_ref[...],
                         jnp.where(i < n_tiles - 1, xnext_ref[...], 0.0)], axis=0)
    hb = (x * lax.rsqrt(jnp.mean(x * x, axis=-1, keepdims=True) + EPS) * g_ref[...]).astype(BF16)
    valid = _positions(i, tm, lp, (tm, 1), 0) < seq_len
    n_blocks = 2 * D_MP // QK_BLOCK

    def project(j):
        zs_ref[j % 2] = jnp.dot(hb, wqk_ref[:, j * QK_BLOCK:(j + 1) * QK_BLOCK], preferred_element_type=F32)

    def activate(j):
        sl = slice(j * QK_BLOCK, (j + 1) * QK_BLOCK)
        z = zs_ref[j % 2]
        n_ext = z.shape[0]
        y = (cw_ref[0:1, sl] * pltpu.roll(z, 1, 0) + cw_ref[1:2, sl] * z
             + cw_ref[2:3, sl] * pltpu.roll(z, n_ext - 1, 0) + cb_ref[:, sl])[XHALO:XHALO + tm]
        y = jnp.where(valid, _silu(y), 0.0)
        if j * QK_BLOCK < D_MP:
            q_ref[:, sl] = (y * (HEAD_DIM ** -0.5)).astype(BF16)
        else:
            k_ref[:, j * QK_BLOCK - D_MP:(j + 1) * QK_BLOCK - D_MP] = y.astype(BF16)

    project(0)
    for j in range(n_blocks):
        if j + 1 < n_blocks:
            project(j + 1)
        else:
            hc = hb[XHALO:XHALO + tm]
            for w_ref, out_ref in ((wu_ref, u_ref), (wv_ref, v_ref), (wo_ref, o_ref), (wg_ref, gt_ref)):
                out_ref[...] = jnp.dot(hc, w_ref[...], preferred_element_type=F32).astype(out_ref.dtype)
        activate(j)


def _inproj(x, g, wu, wqk, wv, wo, wg, cw, cb, lp, seq_len):
    n = x.shape[0]
    tm = ROW_TILE
    nt = n // tm
    hpt = tm // XHALO
    row = lambda w: pl.BlockSpec((tm, w), lambda i: (i, 0))
    prev = pl.BlockSpec((XHALO, D_MODEL), lambda i: (jnp.maximum(i * hpt - 1, 0), 0))
    nxt = pl.BlockSpec((XHALO, D_MODEL), lambda i: (jnp.minimum((i + 1) * hpt, n // XHALO - 1), 0))
    consts = (g, wu, wqk, wv, wo, wg, cw, cb)
    outs = ((D_POOL, F32), (D_MP, BF16), (D_MP, BF16), (D_MP, BF16), (D_MP, F32), (N_GATE, F32))
    return pl.pallas_call(
        functools.partial(_inproj_kernel, lp, seq_len, nt),
        name="inproj",
        grid=(nt,),
        in_specs=[row(D_MODEL), prev, nxt] + [_const_spec(w.shape) for w in consts],
        out_specs=[row(w) for w, _ in outs],
        out_shape=[jax.ShapeDtypeStruct((n, w), dt) for w, dt in outs],
        scratch_shapes=[pltpu.VMEM((2, tm + 2 * XHALO, QK_BLOCK), F32)],
        compiler_params=_cparams(("parallel",)),
    )(x, x, x, *consts)


def _log_sigmoid(x):
    return jnp.minimum(x, 0.0) - jnp.log1p(jnp.exp(-jnp.abs(x)))


def _mlstm_chunk(rev, valid, q_ref, k_ref, v_ref, gates, h_ref, c_ref, n_ref, m_ref):
    t_i = lax.broadcasted_iota(I32, (CHUNK, CHUNK), 0)
    s_i = lax.broadcasted_iota(I32, (CHUNK, CHUNK), 1)
    eye = t_i == s_i
    tri = (s_i >= t_i) if rev else (s_i <= t_i)
    tri_t = (t_i >= s_i) if rev else (t_i <= s_i)

    def to_row(col):
        return jnp.sum(jnp.where(eye, col, 0.0), axis=0, keepdims=True)

    log_i = jnp.where(valid, gates, NEG)
    log_f = jnp.where(valid, _log_sigmoid(gates), 0.0)
    for h in range(N_HEADS):
        sl = slice(h * HEAD_PAD, (h + 1) * HEAD_PAD)
        st = (N_HEADS if rev else 0) + h
        ji = (N_HEADS if rev else 0) + h
        jf = (3 * N_HEADS if rev else 2 * N_HEADS) + h
        li_c = log_i[:, ji:ji + 1]
        lf_c = log_f[:, jf:jf + 1]
        li_r = to_row(li_c)
        lf_r = to_row(lf_c)
        b_c = jnp.sum(jnp.where(tri, lf_r, 0.0), axis=1, keepdims=True)
        b_r = jnp.sum(jnp.where(tri_t, lf_c, 0.0), axis=0, keepdims=True)
        b_tot = jnp.sum(lf_c, axis=0, keepdims=True)
        m_prev = m_ref[st]
        n_prev = n_ref[st]

        a_c = b_tot - b_c + li_c
        m_loc = jnp.max(a_c, axis=0, keepdims=True)
        wa_c = jnp.exp(a_c - m_loc)

        dlog = jnp.where(tri, b_c - b_r + li_r, -jnp.inf)
        inter = b_c + m_prev
        m_t = jnp.maximum(inter, jnp.max(dlog, axis=1, keepdims=True))
        w_inter = jnp.exp(inter - m_t)
        dexp = jnp.exp(dlog - m_t)

        qb = q_ref[:, sl]
        kb = k_ref[:, sl]
        vb = v_ref[:, sl]
        scores = lax.dot_general(qb, kb, (((1,), (1,)), ((), ())), preferred_element_type=F32) * dexp
        c_prev = jnp.concatenate([c_ref[st, :HEAD_DIM, :].astype(BF16),
                                  jnp.zeros((HEAD_PAD - HEAD_DIM, HEAD_PAD), BF16)], axis=0)
        num = (w_inter * jnp.dot(qb, c_prev, preferred_element_type=F32)
               + jnp.dot(scores.astype(BF16), vb, preferred_element_type=F32))
        den = (w_inter * jnp.sum(qb.astype(F32) * n_prev, axis=1, keepdims=True)
               + jnp.sum(scores, axis=1, keepdims=True))
        hh = num / jnp.maximum(jnp.abs(den), jnp.exp(-m_t))
        h_ref[:, sl] = jnp.where(valid, hh, 0.0)

        m_new = jnp.maximum(b_tot + m_prev, m_loc)
        s_prev = jnp.exp(b_tot + m_prev - m_new)
        s_loc = jnp.exp(m_loc - m_new)
        kw = kb.astype(F32) * wa_c
        c_loc = lax.dot_general(kw.astype(BF16), vb, (((0,), (0,)), ((), ())), preferred_element_type=F32)
        c_ref[st, :HEAD_DIM, :] = s_prev * c_ref[st, :HEAD_DIM, :] + s_loc * c_loc[:HEAD_DIM]
        n_ref[st] = s_prev * n_prev + s_loc * jnp.sum(kw, axis=0, keepdims=True)
        m_ref[st] = m_new


def _mlstm_kernel(seq_len, n_chunks, qf_ref, kf_ref, vf_ref, gf_ref, qb_ref, kb_ref, vb_ref, gb_ref, bias_ref,
                  hf_ref, hb_ref, c_ref, n_ref, m_ref):
    c = pl.program_id(1)

    @pl.when(c == 0)
    def _():
        c_ref[...] = jnp.zeros_like(c_ref)
        n_ref[...] = jnp.zeros_like(n_ref)
        m_ref[...] = jnp.zeros_like(m_ref)

    row = lax.broadcasted_iota(I32, (CHUNK, 1), 0)
    _mlstm_chunk(False, (c * CHUNK + row) < seq_len, qf_ref, kf_ref, vf_ref, gf_ref[...] + bias_ref[...],
                 hf_ref, c_ref, n_ref, m_ref)
    _mlstm_chunk(True, ((n_chunks - 1 - c) * CHUNK + row) < seq_len, qb_ref, kb_ref, vb_ref,
                 gb_ref[...] + bias_ref[...], hb_ref, c_ref, n_ref, m_ref)


def _mlstm(q, k, v, gates, bias, n_batch, lp, seq_len):
    n = q.shape[0]
    nc = lp // CHUNK
    fwd = lambda w: pl.BlockSpec((CHUNK, w), lambda b, c: (b * nc + c, 0))
    bwd = lambda w: pl.BlockSpec((CHUNK, w), lambda b, c: (b * nc + nc - 1 - c, 0))
    return pl.pallas_call(
        functools.partial(_mlstm_kernel, seq_len, nc),
        name="mlstm",
        grid=(n_batch, nc),
        in_specs=[fwd(D_MP), fwd(D_MP), fwd(D_MP), fwd(N_GATE), bwd(D_MP), bwd(D_MP), bwd(D_MP), bwd(N_GATE),
                  _const_spec(bias.shape)],
        out_specs=[fwd(D_MP), bwd(D_MP)],
        out_shape=[jax.ShapeDtypeStruct((n, D_MP), F32), jax.ShapeDtypeStruct((n, D_MP), F32)],
        scratch_shapes=[pltpu.VMEM((2 * N_HEADS, HEAD_PAD, HEAD_PAD), F32),
                        pltpu.VMEM((2 * N_HEADS, 1, HEAD_PAD), F32),
                        pltpu.VMEM((2 * N_HEADS, 1, 1), F32)],
        compiler_params=_cparams(("parallel", "arbitrary")),
    )(q, k, v, gates, q, k, v, gates, bias)


def _positions(i, tm, lp, shape, axis):
    row0 = i * tm
    p0 = row0 - (row0 // lp) * lp
    p = p0 + lax.broadcasted_iota(I32, shape, axis)
    return jnp.where(p >= lp, p - lp, p)


def _mixout_kernel(lp, seq_len, n_tiles,
                   u_ref, uprev_ref, unext_ref, hf_ref, hb_ref, o_ref, x_ref,
                   pw_ref, ps_ref, mg_ref, wop_ref, wom_ref, g2_ref, wrh_ref, wrl_ref,
                   x1_ref, aug_ref, afft_ref):
    i = pl.program_id(0)
    tm = x_ref.shape[0]
    pos = _positions(i, tm, lp, (tm, 1), 0)
    valid = pos < seq_len

    u = u_ref[...]
    ext = jnp.concatenate([jnp.where(i > 0, uprev_ref[...], 0.0), u,
                           jnp.where(i < n_tiles - 1, unext_ref[...], 0.0)], axis=0)
    n_ext = tm + 2 * HALO

    def shifted(a, d):
        return pltpu.roll(a, (-d) % n_ext, 0)

    w2 = shifted(ext, -1) + ext
    w4 = shifted(w2, -1) + shifted(w2, 1)
    w8 = shifted(w4, -2) + shifted(w4, 2)
    w16 = shifted(w8, -4) + shifted(w8, 4)
    lane = lax.broadcasted_iota(I32, (tm, D_POOL), 1)
    grp = lane // D_POOL_GROUP
    centre = slice(HALO, HALO + tm)
    wsum = jnp.where(grp == 0, w2[centre], jnp.where(grp == 1, w4[centre],
                     jnp.where(grp == 2, w8[centre], w16[centre])))
    half = jnp.where(grp == 0, 1, jnp.where(grp == 1, 2, jnp.where(grp == 2, 4, 8)))
    lo = jnp.clip(pos - half, 0, seq_len)
    hi = jnp.clip(pos + half, 0, seq_len)
    cnt = jnp.maximum(hi - lo, 1).astype(F32)
    mixed = jnp.where(valid, wsum / cnt - u, 0.0)
    y_pool = jnp.dot(mixed.astype(BF16), pw_ref[...], preferred_element_type=F32) * ps_ref[...]

    hs = hf_ref[...] + hb_ref[...]
    og = _sigmoid(o_ref[...])
    parts = []
    for h in range(N_HEADS):
        sl = slice(h * HEAD_PAD, (h + 1) * HEAD_PAD)
        hh = hs[:, sl]
        ms = jnp.sum(hh * hh, axis=-1, keepdims=True) * (1.0 / HEAD_DIM)
        parts.append(hh * lax.rsqrt(ms + EPS))
    y_m = jnp.concatenate(parts, axis=1) * mg_ref[...] * og
    y_m = jnp.where(valid, y_m, 0.0)

    x1 = (x_ref[...] + jnp.dot(y_pool.astype(BF16), wop_ref[...], preferred_element_type=F32)
          + jnp.dot(y_m.astype(BF16), wom_ref[...], preferred_element_type=F32))
    x1_ref[...] = x1

    h2 = x1 * lax.rsqrt(jnp.mean(x1 * x1, axis=-1, keepdims=True) + EPS) * g2_ref[...]
    h_hi = h2.astype(BF16)
    h_lo = (h2 - h_hi.astype(F32)).astype(BF16)
    logits = (jnp.dot(h_hi, wrh_ref[...], preferred_element_type=F32)
              + jnp.dot(h_lo, wrh_ref[...], preferred_element_type=F32)
              + jnp.dot(h_hi, wrl_ref[...], preferred_element_type=F32))
    lane = lax.broadcasted_iota(I32, (tm, LANES), 1)
    logits = jnp.where(lane < N_EXPERTS, logits, -jnp.inf)
    e = jnp.exp(logits - jnp.max(logits, axis=-1, keepdims=True))
    aff = e / jnp.sum(e, axis=-1, keepdims=True)
    hi = aff.astype(BF16).astype(F32)
    mid = (aff - hi).astype(BF16).astype(F32)
    lo = aff - hi - mid
    parts = (hi + pltpu.roll(mid, N_EXPERTS, 1) + pltpu.roll(lo, 2 * N_EXPERTS, 1)).astype(BF16)
    aug_ref[:, :D_MODEL] = h_hi
    aug_ref[:, D_MODEL:] = parts

    e_i = lax.broadcasted_iota(I32, (N_EXPERTS, LANES), 0)
    l_i = lax.broadcasted_iota(I32, (N_EXPERTS, LANES), 1)
    pick = jnp.logical_and((l_i & (N_EXPERTS - 1)) == e_i, l_i < AFF_PARTS * N_EXPERTS).astype(BF16)
    aff_t = lax.dot_general(pick, parts, (((1,), (1,)), ((), ())), preferred_element_type=F32)
    valid_r = _positions(i, tm, lp, (1, tm), 1) < seq_len
    afft_ref[...] = jnp.where(valid_r, aff_t, -1.0)


def _mixout(u, hf, hb, o, x, pw, ps, mg, wop, wom, g2, wrh, wrl, lp, seq_len):
    n = x.shape[0]
    tm = MIX_TILE
    nt = n // tm
    hpt = tm // HALO
    row = lambda w: pl.BlockSpec((tm, w), lambda i: (i, 0))
    prev = pl.BlockSpec((HALO, D_POOL), lambda i: (jnp.maximum(i * hpt - 1, 0), 0))
    nxt = pl.BlockSpec((HALO, D_POOL), lambda i: (jnp.minimum((i + 1) * hpt, n // HALO - 1), 0))
    consts = (pw, ps, mg, wop, wom, g2, wrh, wrl)
    return pl.pallas_call(
        functools.partial(_mixout_kernel, lp, seq_len, nt),
        name="mixout",
        grid=(nt,),
        in_specs=[row(D_POOL), prev, nxt, row(D_MP), row(D_MP), row(D_MP), row(D_MODEL)]
                 + [_const_spec(w.shape) for w in consts],
        out_specs=[row(D_MODEL), row(AUG), pl.BlockSpec((N_EXPERTS, tm), lambda i: (0, i))],
        out_shape=[jax.ShapeDtypeStruct((n, D_MODEL), F32), jax.ShapeDtypeStruct((n, AUG), BF16),
                   jax.ShapeDtypeStruct((N_EXPERTS, n), F32)],
        compiler_params=_cparams(("parallel",)),
    )(u, u, u, hf, hb, o, x, *consts)


def _topk_kernel(cap, aff_ref, mask_ref, bpx_ref):
    ne, nb, _ = aff_ref.shape
    bits = lax.bitcast_convert_type(aff_ref[...], I32)

    def count(pred):
        return jnp.sum(jnp.sum(pred.astype(I32), axis=2, keepdims=True), axis=1, keepdims=True)

    def bisect(it, thr):
        cand = thr | (jnp.int32(1) << (30 - it))
        return jnp.where(count(bits >= cand) >= cap, cand, thr)

    thr = lax.fori_loop(0, 31, bisect, jnp.zeros((ne, 1, 1), I32))
    gt = bits > thr
    eq = bits == thr
    need = cap - count(gt)

    i_a = lax.broadcasted_iota(I32, (LANES, LANES), 0)
    i_b = lax.broadcasted_iota(I32, (LANES, LANES), 1)
    upper = (i_a <= i_b).astype(BF16)
    k_a = lax.broadcasted_iota(I32, (nb, nb), 0)
    k_b = lax.broadcasted_iota(I32, (nb, nb), 1)
    strict_lower = (k_b < k_a).astype(BF16)

    def block_prefix(local_incl):
        totals = jnp.broadcast_to(local_incl[:, LANES - 1:LANES], (nb, LANES)).astype(BF16)
        return jnp.dot(strict_lower, totals, preferred_element_type=F32)

    for e in range(ne):
        eq_e = eq[e].astype(BF16)
        eq_rank = jnp.dot(eq_e, upper, preferred_element_type=F32)
        eq_rank = eq_rank + block_prefix(eq_rank)
        need_e = need[e].astype(F32)
        sel = jnp.logical_or(gt[e], jnp.logical_and(eq[e], eq_rank <= need_e)).astype(BF16)
        mask_ref[e] = sel
        bpx_ref[e] = block_prefix(jnp.dot(sel, upper, preferred_element_type=F32)).astype(I32)


def _topk(aff_t, cap):
    ne, n = aff_t.shape
    nb = n // LANES
    nbp = -(-nb // LANES) * LANES
    aff3 = jnp.pad(aff_t.reshape(ne, nb, LANES), ((0, 0), (0, nbp - nb), (0, 0)), constant_values=-1.0)
    mask, bpx = pl.pallas_call(
        functools.partial(_topk_kernel, cap),
        name="topk",
        grid=(1,),
        in_specs=[_const_spec(aff3.shape)],
        out_specs=[_const_spec(aff3.shape), _const_spec(aff3.shape)],
        out_shape=[jax.ShapeDtypeStruct(aff3.shape, BF16), jax.ShapeDtypeStruct(aff3.shape, I32)],
        compiler_params=_cparams(("arbitrary",)),
    )(aff3)
    starts = bpx[:, 0:nb:DISPATCH_TILE // LANES, 0]
    counts = jnp.concatenate([starts[:, 1:], jnp.full((ne, 1), cap, I32)], axis=1) - starts
    return mask.reshape(ne, nbp * LANES), starts.T.reshape(-1), counts.T.reshape(-1)


def _tile_ranks(m):
    t = m.shape[1]
    before = (lax.broadcasted_iota(I32, (t, t), 0) < lax.broadcasted_iota(I32, (t, t), 1)).astype(BF16)
    rank = jnp.dot(m, before, preferred_element_type=F32)
    return jnp.where(m > 0, rank, -1.0)


def _f(x):
    return jnp.asarray(x, I32).astype(F32)


def _rank_rows(r, n_rows, off, lo=0, hi=None):
    w = lax.broadcasted_iota(I32, (n_rows, r.shape[1]), 0).astype(F32)
    hit = jnp.logical_and(r == w + _f(off), r >= _f(lo))
    if hi is not None:
        hit = jnp.logical_and(hit, r < _f(hi))
    return hit


def _smax(xs):
    most = xs[0]
    for x in xs[1:]:
        most = jnp.maximum(most, x)
    return most


def _dispatch_kernel(stride, pad, n_tiles, a_ref, c_ref, aug_ref, mask_ref, xe_hbm, xc, xb, xo, carry, sem, sem_o):
    win = SLAB + ALIGN
    per = win + ALIGN
    i = pl.program_id(0)
    slot = i % 2
    start = [a_ref[i * N_EXPERTS + e] for e in range(N_EXPERTS)]
    count = [c_ref[i * N_EXPERTS + e] for e in range(N_EXPERTS)]
    head = [a & (ALIGN - 1) for a in start]
    base = [a - h for a, h in zip(start, head)]
    end = [a + c for a, c in zip(start, count)]
    head2 = [x & (ALIGN - 1) for x in end]
    base2 = [x - h for x, h in zip(end, head2)]
    rankm = _tile_ranks(mask_ref[...])
    aug = aug_ref[...]

    @pl.when(i == 0)
    def _():
        carry[...] = jnp.zeros_like(carry)
        xb[0] = jnp.zeros(xb.shape[1:], xb.dtype)
        pads = [pltpu.make_async_copy(xb.at[0, pl.ds(0, pad)], xe_hbm.at[pl.ds(e * stride + stride - pad, pad)], sem_o)
                for e in range(N_EXPERTS)]
        for cp in pads:
            cp.start()
        for cp in pads:
            cp.wait()

    parts = []
    for e in range(N_EXPERTS):
        r = rankm[e:e + 1, :]
        sub = lax.broadcasted_iota(I32, (ALIGN, r.shape[1]), 0)
        nxt = jnp.logical_and(_rank_rows(r, ALIGN, base2[e] - start[e]), sub < head2[e])
        parts += [_rank_rows(r, win, -head[e]).astype(F32), nxt.astype(F32)]
    xc[...] = jnp.dot(jnp.concatenate(parts, axis=0).astype(BF16), aug, preferred_element_type=F32)
    for e in range(N_EXPERTS):
        old = carry[e * ALIGN:(e + 1) * ALIGN, :]
        carry[e * ALIGN:(e + 1) * ALIGN, :] = (xc[e * per + win:(e + 1) * per, :]
                                               + jnp.where(base2[e] == base[e], old, 0.0))
        xb[slot, pl.ds(e * win, ALIGN), :] = (xc[e * per:e * per + ALIGN, :] + old).astype(xb.dtype)
        xb[slot, pl.ds(e * win + ALIGN, SLAB), :] = xc[e * per + ALIGN:e * per + win, :].astype(xb.dtype)

    def wait_slot(sl):
        pltpu.make_async_copy(xb.at[sl], xe_hbm.at[pl.ds(0, N_EXPERTS * win)], sem.at[sl]).wait()

    @pl.when(i > 0)
    def _():
        wait_slot(1 - slot)

    for e in range(N_EXPERTS):
        dst = e * stride + pl.multiple_of(base[e], ALIGN)
        pltpu.make_async_copy(xb.at[slot, pl.ds(e * win, win)], xe_hbm.at[pl.ds(dst, win)], sem.at[slot]).start()

    def extra_copy(e, k):
        dst = e * stride + pl.multiple_of(base[e] + win + SLAB * (k - 1), ALIGN)
        return pltpu.make_async_copy(xo.at[pl.ds(e * SLAB, SLAB)], xe_hbm.at[pl.ds(dst, SLAB)], sem_o)

    def extra_window(k, _):
        p = [_rank_rows(rankm[e:e + 1, :], SLAB, win - head[e] + SLAB * (k - 1)).astype(F32)
             for e in range(N_EXPERTS)]
        xo[...] = jnp.dot(jnp.concatenate(p, axis=0).astype(BF16), aug, preferred_element_type=F32).astype(xo.dtype)
        for e in range(N_EXPERTS):
            @pl.when(count[e] + head[e] > win + SLAB * (k - 1))
            def _():
                extra_copy(e, k).start()
        for e in range(N_EXPERTS):
            @pl.when(count[e] + head[e] > win + SLAB * (k - 1))
            def _():
                extra_copy(e, k).wait()
        return 0

    most = _smax([c + h for c, h in zip(count, head)])
    n_extra = lax.div(jnp.maximum(most - win, 0) + (SLAB - 1), jnp.int32(SLAB))
    lax.fori_loop(1, n_extra + 1, extra_window, 0)

    @pl.when(i == n_tiles - 1)
    def _():
        wait_slot(slot)


def _dispatch(starts, counts, aug, mask, stride, pad):
    win = SLAB + ALIGN
    assert SLAB % ALIGN == 0 and pad <= N_EXPERTS * win
    n = aug.shape[0]
    t = DISPATCH_TILE
    nt = n // t
    grid_spec = pltpu.PrefetchScalarGridSpec(
        num_scalar_prefetch=2,
        grid=(nt,),
        in_specs=[pl.BlockSpec((t, AUG), lambda i, a, c: (i, 0)),
                  pl.BlockSpec((N_EXPERTS, t), lambda i, a, c: (0, i))],
        out_specs=pl.BlockSpec(memory_space=pl.ANY),
        scratch_shapes=[pltpu.VMEM((N_EXPERTS * (win + ALIGN), AUG), F32), pltpu.VMEM((2, N_EXPERTS * win, AUG), BF16),
                        pltpu.VMEM((N_EXPERTS * SLAB, AUG), BF16), pltpu.VMEM((N_EXPERTS * ALIGN, AUG), F32),
                        pltpu.SemaphoreType.DMA((2,)), pltpu.SemaphoreType.DMA(())],
    )
    return pl.pallas_call(
        functools.partial(_dispatch_kernel, stride, pad, nt),
        name="dispatch",
        grid_spec=grid_spec,
        out_shape=jax.ShapeDtypeStruct((N_EXPERTS * stride, AUG), BF16),
        compiler_params=_cparams(("arbitrary",), has_side_effects=True),
    )(starts, counts, aug, mask)


def _ffn_kernel(xe_ref, wg_ref, wu_ref, wd_ref, ye_ref, wb_ref):
    e = pl.program_id(0)
    xb = xe_ref[:, :D_MODEL]
    lane = lax.broadcasted_iota(I32, (xb.shape[0], LANES), 1)
    mine = jnp.logical_and((lane & (N_EXPERTS - 1)) == e, lane < AFF_PARTS * N_EXPERTS)
    gate = jnp.sum(jnp.where(mine, xe_ref[:, D_MODEL:].astype(F32), 0.0), axis=1, keepdims=True)

    @pl.when(pl.program_id(1) == 0)
    def _():
        wb_ref[0] = wg_ref[0].astype(BF16)
        wb_ref[1] = wu_ref[0].astype(BF16)
        wb_ref[2] = wd_ref[0].astype(BF16)

    hid = (_silu(jnp.dot(xb, wb_ref[0], preferred_element_type=F32))
           * jnp.dot(xb, wb_ref[1], preferred_element_type=F32))
    ye_ref[...] = (jnp.dot(hid.astype(BF16), wb_ref[2], preferred_element_type=F32) * gate).astype(ye_ref.dtype)


def _ffn(xe, wg, wu, wd, rows, npe):
    wspec = pl.BlockSpec((1, D_MODEL, D_MODEL), lambda e, i: (e, 0, 0))
    return pl.pallas_call(
        _ffn_kernel,
        name="expert_ffn",
        grid=(N_EXPERTS, npe),
        in_specs=[pl.BlockSpec((rows, AUG), lambda e, i: (e * (npe + 1) + i, 0)), wspec, wspec, wspec],
        out_specs=pl.BlockSpec((rows, D_MODEL), lambda e, i: (e * npe + i, 0)),
        out_shape=jax.ShapeDtypeStruct((N_EXPERTS * npe * rows, D_MODEL), BF16),
        scratch_shapes=[pltpu.VMEM((3, D_MODEL, D_MODEL), BF16)],
        compiler_params=_cparams(("parallel", "arbitrary"), vmem=FFN_VMEM_LIMIT),
    )(xe, wg, wu, wd)


def _combine_kernel(stride, n_tiles, a_ref, c_ref, x1_ref, mask_ref, ye_hbm, out_ref, yc, yo, sem, sem_o):
    win = SLAB + ALIGN
    i = pl.program_id(0)
    slot = i % 2

    def window_start(tile, e, k):
        first = a_ref[tile * N_EXPERTS + e] + SLAB * k
        return pl.multiple_of(jnp.minimum(first - (first & (ALIGN - 1)), stride - win), ALIGN)

    def window_copy(tile, e, k, buf, s):
        return pltpu.make_async_copy(ye_hbm.at[pl.ds(e * stride + window_start(tile, e, k), win)],
                                     buf.at[pl.ds(e * win, win)], s)

    def wait_all(buf, s):
        pltpu.make_async_copy(ye_hbm.at[pl.ds(0, N_EXPERTS * win)], buf, s).wait()

    @pl.when(i == 0)
    def _():
        for e in range(N_EXPERTS):
            window_copy(i, e, 0, yc.at[slot], sem.at[slot]).start()

    @pl.when(i + 1 < n_tiles)
    def _():
        for e in range(N_EXPERTS):
            window_copy(i + 1, e, 0, yc.at[1 - slot], sem.at[1 - slot]).start()

    starts = [a_ref[i * N_EXPERTS + e] for e in range(N_EXPERTS)]
    counts = [c_ref[i * N_EXPERTS + e] for e in range(N_EXPERTS)]
    rankm = _tile_ranks(mask_ref[...])

    def scattered(k, y):
        p = jnp.concatenate(
            [_rank_rows(rankm[e:e + 1, :], win, window_start(i, e, k) - starts[e], SLAB * k, SLAB * (k + 1))
             .astype(F32) for e in range(N_EXPERTS)], axis=0).astype(BF16)
        return lax.dot_general(p, y, (((0,), (0,)), ((), ())), preferred_element_type=F32)

    wait_all(yc.at[slot], sem.at[slot])
    out_ref[...] = x1_ref[...] + scattered(0, yc[slot])

    def extra_window(k, carry):
        for e in range(N_EXPERTS):
            window_copy(i, e, k, yo, sem_o).start()
        wait_all(yo, sem_o)
        out_ref[...] += scattered(k, yo[...])
        return carry

    lax.fori_loop(1, lax.div(_smax(counts) + (SLAB - 1), jnp.int32(SLAB)), extra_window, 0)


def _combine(starts, counts, x1, mask, ye, stride):
    n = x1.shape[0]
    t = DISPATCH_TILE
    nt = n // t
    win = SLAB + ALIGN
    row = pl.BlockSpec((t, D_MODEL), lambda i, a, c: (i, 0))
    grid_spec = pltpu.PrefetchScalarGridSpec(
        num_scalar_prefetch=2,
        grid=(nt,),
        in_specs=[row, pl.BlockSpec((N_EXPERTS, t), lambda i, a, c: (0, i)), pl.BlockSpec(memory_space=pl.ANY)],
        out_specs=row,
        scratch_shapes=[pltpu.VMEM((2, N_EXPERTS * win, D_MODEL), BF16), pltpu.VMEM((N_EXPERTS * win, D_MODEL), BF16),
                        pltpu.SemaphoreType.DMA((2,)), pltpu.SemaphoreType.DMA(())],
    )
    return pl.pallas_call(
        functools.partial(_combine_kernel, stride, nt),
        name="combine",
        grid_spec=grid_spec,
        out_shape=jax.ShapeDtypeStruct(x1.shape, F32),
        compiler_params=_cparams(("arbitrary",)),
    )(starts, counts, x1, mask, ye)


def _moe(x1, aug, aff_t, wg, wu, wd, cap):
    win = SLAB + ALIGN
    rows = min(range(win, MAX_FFN_ROWS + 1, ALIGN), key=lambda r: -(-cap // r) * (r + FFN_STEP_ROWS))
    npe = -(-cap // rows)
    stride = (npe + 1) * rows
    mask, starts, counts = _topk(aff_t, cap)
    xe = _dispatch(starts, counts, aug, mask, stride, stride - cap // ALIGN * ALIGN)
    ye = _ffn(xe, wg, wu, wd, rows, npe)
    return _combine(starts, counts, x1, mask, ye, npe * rows)


def _final_kernel(lp, tiles_per_seq, x_hbm, g_ref, y_ref, buf, sem):
    i = pl.program_id(0)
    n = pl.num_programs(0)
    tm = buf.shape[1]
    slot = i % 2

    def fetch(step, sl):
        b = lax.div(step, jnp.int32(tiles_per_seq))
        src = pl.multiple_of(b * lp + N_META + (step - b * tiles_per_seq) * tm, SUBLANES)
        return pltpu.make_async_copy(x_hbm.at[pl.ds(src, tm)], buf.at[sl], sem.at[sl])

    @pl.when(i == 0)
    def _():
        fetch(i, slot).start()

    @pl.when(i + 1 < n)
    def _():
        fetch(i + 1, 1 - slot).start()

    fetch(i, slot).wait()
    x = buf[slot]
    y_ref[...] = x * lax.rsqrt(jnp.mean(x * x, axis=-1, keepdims=True) + EPS) * g_ref[...]


def _final_norm(x, g, n_batch, lp, s):
    tm = max(r for r in range(SUBLANES, ROW_TILE + 1, SUBLANES) if s % r == 0)
    tiles_per_seq = s // tm
    return pl.pallas_call(
        functools.partial(_final_kernel, lp, tiles_per_seq),
        name="final_norm",
        grid=(n_batch * tiles_per_seq,),
        in_specs=[pl.BlockSpec(memory_space=pl.ANY), _const_spec(g.shape)],
        out_specs=pl.BlockSpec((tm, D_MODEL), lambda i: (i, 0)),
        out_shape=jax.ShapeDtypeStruct((n_batch * s, D_MODEL), F32),
        scratch_shapes=[pltpu.VMEM((2, tm, D_MODEL), F32), pltpu.SemaphoreType.DMA((2,))],
        compiler_params=_cparams(("arbitrary",)),
    )(x, g)


def _pad_heads(w, axis):
    shape = w.shape
    w = w.reshape(shape[:axis] + (N_HEADS, HEAD_DIM) + shape[axis + 1:])
    pad = [(0, 0)] * w.ndim
    pad[axis + 1] = (0, HEAD_PAD - HEAD_DIM)
    return jnp.pad(w, pad).reshape(shape[:axis] + (D_MP,) + shape[axis + 1:])


def _prepare(norm1_g, w_in, conv_w, conv_b, gate_b, pool_w, pool_scale, mh_norm_g, w_out, norm2_g,
             w_router, w_gate, w_up, w_down, final_g):
    depth = w_in.shape[0]
    o1 = D_POOL
    offs = [o1 + j * D_MLSTM for j in range(5)]
    layers = []
    for l in range(depth):
        w = w_in[l]
        proj = [_pad_heads(w[:, offs[j]:offs[j + 1]], 1).astype(BF16) for j in range(4)]
        eye = jnp.eye(len(POOL_WINDOWS), dtype=F32)
        pw = (eye[:, None, :, None] * pool_w[l][:, :, None, :]).reshape(D_POOL, D_POOL)
        wr = jnp.pad(w_router[l], ((0, 0), (0, LANES - N_EXPERTS)))
        wr_hi = wr.astype(BF16)
        layers.append(dict(
            g1=norm1_g[l][None], wu=w[:, :o1].astype(BF16), wqk=jnp.concatenate(proj[:2], axis=1),
            wv=proj[2], wo=proj[3], wgate=w[:, offs[4]:].astype(BF16),
            cw=jnp.concatenate([_pad_heads(conv_w[l][:, :D_MLSTM], 1), _pad_heads(conv_w[l][:, D_MLSTM:], 1)], axis=1),
            cb=jnp.concatenate([_pad_heads(conv_b[l][:D_MLSTM], 0), _pad_heads(conv_b[l][D_MLSTM:], 0)])[None],
            gb=gate_b[l].reshape(1, N_GATE),
            pw=pw.astype(BF16), ps=pool_scale[l][None], mg=_pad_heads(mh_norm_g[l], 0)[None],
            wop=w_out[l][:D_POOL].astype(BF16), wom=_pad_heads(w_out[l][D_POOL:], 0).astype(BF16),
            g2=norm2_g[l][None], wrh=wr_hi, wrl=(wr - wr_hi.astype(F32)).astype(BF16),
            eg=w_gate[l], eu=w_up[l], ed=w_down[l],
        ))
    return layers, final_g[None]


def _trunk(x_in, meta_tokens, layers, final_g):
    n_batch, s, _ = x_in.shape
    seq_len = s + N_META
    lp = -(-(seq_len + HALO) // CHUNK) * CHUNK
    n_tok = n_batch * seq_len
    cap = CAPACITY_FACTOR * n_tok // N_EXPERTS
    meta = jnp.broadcast_to(meta_tokens[None].astype(x_in.dtype), (n_batch, N_META, D_MODEL))
    x = jnp.concatenate([meta, x_in, jnp.zeros((n_batch, lp - seq_len, D_MODEL), x_in.dtype)], axis=1)
    x = x.reshape(n_batch * lp, D_MODEL)
    for p in layers:
        u, q, k, v, o, gates = _inproj(x, p["g1"], p["wu"], p["wqk"], p["wv"], p["wo"], p["wgate"],
                                       p["cw"], p["cb"], lp, seq_len)
        hf, hb = _mlstm(q, k, v, gates, p["gb"], n_batch, lp, seq_len)
        x1, aug, aff_t = _mixout(u, hf, hb, o, x, p["pw"], p["ps"], p["mg"], p["wop"], p["wom"],
                                 p["g2"], p["wrh"], p["wrl"], lp, seq_len)
        x = _moe(x1, aug, aff_t, p["eg"], p["eu"], p["ed"], cap)
    return _final_norm(x, final_g, n_batch, lp, s).reshape(n_batch, s, D_MODEL)


def kernel(x_prompt, x_sample, meta_tokens, norm1_g, w_in, conv_w, conv_b, gate_b, pool_w, pool_scale,
           mh_norm_g, w_out, norm2_g, w_router, w_gate, w_up, w_down, final_g):
    layers, fg = _prepare(norm1_g, w_in, conv_w, conv_b, gate_b, pool_w, pool_scale, mh_norm_g, w_out,
                          norm2_g, w_router, w_gate, w_up, w_down, final_g)
    return (_trunk(x_prompt, meta_tokens, layers, fg), _trunk(x_sample, meta_tokens, layers, fg))
```
